```python
import jax, jax.numpy as jnp
from jax import lax
import numpy as np

D_MODEL = 1024
BATCH = 32
SEQ = 2048
DEPTH = 1
DEC_BATCH = 2
DEC_SEQ = 8192
PAST_LEN = 128

N_FOURIER_GROUPS = 4
FOURIER_GROUP_DIM = 128
FOURIER_WIDTH = N_FOURIER_GROUPS * FOURIER_GROUP_DIM
N_SGU_GROUPS = 4
SGU_GROUP_DIM = 128
SGU_WIDTH = N_SGU_GROUPS * SGU_GROUP_DIM
CHUNK = 128
D_FF = 4 * D_MODEL
CONV_WIDTH = 3
IN_PROJ_WIDTH = FOURIER_WIDTH + 2 * SGU_WIDTH + 2 * D_MODEL
EPS = 1e-6

kernel_name = "fnet_gmlp_gated_hybrid_encoder"


def rmsnorm(x, g):
    xf = x.astype(jnp.float32)
    r = lax.rsqrt(jnp.mean(xf * xf, axis=-1, keepdims=True) + EPS)
    return (xf * r * g.astype(jnp.float32)).astype(x.dtype)


def layernorm(x, g, b):
    xf = x.astype(jnp.float32)
    mu = jnp.mean(xf, axis=-1, keepdims=True)
    var = jnp.mean(jnp.square(xf - mu), axis=-1, keepdims=True)
    y = (xf - mu) * lax.rsqrt(var + EPS) * g.astype(jnp.float32) + b.astype(jnp.float32)
    return y.astype(x.dtype)


def fourier_mix(f):
    b, s, _ = f.shape
    z = f.reshape(b, s, N_FOURIER_GROUPS, FOURIER_GROUP_DIM).astype(jnp.float32)
    z = jnp.fft.fft2(z, axes=(1, 3), norm="ortho").real
    return z.reshape(b, s, FOURIER_WIDTH).astype(f.dtype)


def spatial_gating(u, v, ln_g, ln_b, w_s, b_s):
    b, s, _ = v.shape
    v = layernorm(v, ln_g, ln_b)
    vc = v.reshape(b, s // CHUNK, CHUNK, N_SGU_GROUPS, SGU_GROUP_DIM)
    mixed = jnp.einsum('hpq,bnqhc->bnphc', w_s, vc) + b_s.T[:, :, None]
    return u * mixed.reshape(b, s, SGU_WIDTH)


def centred_depthwise_conv(x, w, bias):
    xp = jnp.pad(x, ((0, 0), (1, 1), (0, 0)))
    return xp[:, :-2] * w[0] + xp[:, 1:-1] * w[1] + xp[:, 2:] * w[2] + bias


def layer(x, norm_pre_mix, w_in, sgu_ln_g, sgu_ln_b, sgu_w_s, sgu_b_s,
          w_fourier_out, w_sgu_out, w_o, norm_post_mix, norm_pre_ffn,
          w_up, conv_w, conv_b, w_down, norm_post_ffn):
    h = rmsnorm(x, norm_pre_mix)
    proj = jnp.einsum('bsd,de->bse', h, w_in)
    o1 = FOURIER_WIDTH
    o2 = o1 + SGU_WIDTH
    o3 = o2 + SGU_WIDTH
    o4 = o3 + D_MODEL
    f = proj[..., :o1]
    u = jax.nn.gelu(proj[..., o1:o2])
    v = jax.nn.gelu(proj[..., o2:o3])
    gate_a = jax.nn.sigmoid(proj[..., o3:o4])
    gate_b = jax.nn.sigmoid(proj[..., o4:])
    y_a = jnp.einsum('bsc,cd->bsd', fourier_mix(f), w_fourier_out)
    y_b = jnp.einsum('bsc,cd->bsd', spatial_gating(u, v, sgu_ln_g, sgu_ln_b, sgu_w_s, sgu_b_s), w_sgu_out)
    merged = gate_a * y_a + gate_b * y_b
    x = x + rmsnorm(jnp.einsum('bsd,de->bse', merged, w_o), norm_post_mix)

    h2 = rmsnorm(x, norm_pre_ffn)
    up = jnp.einsum('bsd,df->bsf', h2, w_up)
    up = centred_depthwise_conv(up, conv_w, conv_b)
    act = jax.nn.gelu(up[..., :D_FF]) * up[..., D_FF:]
    ff = jnp.einsum('bsf,fd->bsd', act, w_down)
    return x + rmsnorm(ff, norm_post_ffn)


def setup_inputs(seed: int = 0) -> dict:
    key = jax.random.key(seed)
    ks = jax.random.split(key, 20)
    f32 = jnp.float32

    def nrm(k, shape, scale):
        return jax.random.normal(k, shape, f32) * scale

    def gain(k, n):
        return jnp.ones((DEPTH, n), f32) + nrm(k, (DEPTH, n), 0.02)

    return {
        "x_prompt": nrm(ks[0], (BATCH, SEQ, D_MODEL), 1.0),
        "x_sample": nrm(ks[1], (DEC_BATCH, DEC_SEQ, D_MODEL), 1.0),
        "norm_pre_mix": gain(ks[2], D_MODEL),
        "w_in": nrm(ks[3], (DEPTH, D_MODEL, IN_PROJ_WIDTH), D_MODEL ** -0.5),
        "sgu_ln_g": gain(ks[4], SGU_WIDTH),
        "sgu_ln_b": nrm(ks[5], (DEPTH, SGU_WIDTH), 0.02),
        "sgu_w_s": nrm(ks[6], (DEPTH, N_SGU_GROUPS, CHUNK, CHUNK), CHUNK ** -0.5),
        "sgu_b_s": nrm(ks[7], (DEPTH, N_SGU_GROUPS, CHUNK), 0.02),
        "w_fourier_out": nrm(ks[8], (DEPTH, FOURIER_WIDTH, D_MODEL), FOURIER_WIDTH ** -0.5),
        "w_sgu_out": nrm(ks[9], (DEPTH, SGU_WIDTH, D_MODEL), SGU_WIDTH ** -0.5),
        "w_o": nrm(ks[10], (DEPTH, D_MODEL, D_MODEL), D_MODEL ** -0.5),
        "norm_post_mix": gain(ks[11], D_MODEL),
        "norm_pre_ffn": gain(ks[12], D_MODEL),
        "w_up": nrm(ks[13], (DEPTH, D_MODEL, 2 * D_FF), D_MODEL ** -0.5),
        "conv_w": nrm(ks[14], (DEPTH, CONV_WIDTH, 2 * D_FF), CONV_WIDTH ** -0.5),
        "conv_b": nrm(ks[15], (DEPTH, 2 * D_FF), 0.02),
        "w_down": nrm(ks[16], (DEPTH, D_FF, D_MODEL), D_FF ** -0.5),
        "norm_post_ffn": gain(ks[17], D_MODEL),
    }


def reference(x_prompt, x_sample, norm_pre_mix, w_in, sgu_ln_g, sgu_ln_b, sgu_w_s, sgu_b_s,
              w_fourier_out, w_sgu_out, w_o, norm_post_mix, norm_pre_ffn,
              w_up, conv_w, conv_b, w_down, norm_post_ffn):
    y_prompt = x_prompt
    y_sample = x_sample
    for l in range(DEPTH):
        p = (norm_pre_mix[l], w_in[l], sgu_ln_g[l], sgu_ln_b[l], sgu_w_s[l], sgu_b_s[l],
             w_fourier_out[l], w_sgu_out[l], w_o[l], norm_post_mix[l], norm_pre_ffn[l],
             w_up[l], conv_w[l], conv_b[l], w_down[l], norm_post_ffn[l])
        y_prompt = layer(y_prompt, *p)
        y_sample = layer(y_sample, *p)
    return (y_prompt, y_sample)
```

```python
import functools
import math

import jax
import jax.numpy as jnp
from jax import lax
from jax.experimental import pallas as pl
from jax.experimental.pallas import tpu as pltpu

D_MODEL = 1024
N_GROUPS = 4
GROUP_DIM = 128
BRANCH_WIDTH = N_GROUPS * GROUP_DIM
CHUNK = 128
D_FF = 4 * D_MODEL
EPS = 1e-6

O_F, O_U, O_V, O_GA, O_GB = 0, 512, 1024, 1536, 2560

DFT_N2 = 512
VMEM_LIMIT_V7X = 56 * 1024 * 1024

BF16 = jnp.bfloat16
F32 = jnp.float32


def _const_spec(shape):
    nd = len(shape)
    return pl.BlockSpec(shape, lambda *_: (0,) * nd, pipeline_mode=pl.Buffered(1))


def _rms_scale(x):
    return lax.rsqrt(jnp.mean(x * x, axis=-1, keepdims=True) + EPS)


def _gelu_tanh(x):
    k = math.sqrt(2.0 / math.pi)
    inner = x * (k + (k * 0.044715) * (x * x))
    return 0.5 * x * (1.0 + jnp.tanh(inner))


def _sigmoid(x):
    return 1.0 / (1.0 + jnp.exp(-x))


def _dot(a, b):
    return jnp.dot(a, b, preferred_element_type=F32)


def _mix_in_kernel(x_ref, g_ref, win_ref, cs_ref, lng_ref, lnb_ref, ws_ref, bias_ref, wso_ref,
                   zr_ref, zi_ref, sga_ref, pb_ref, *, tm):
    x = x_ref[...]
    h = (x * _rms_scale(x) * g_ref[...]).astype(BF16)

    f = _dot(h, win_ref[:, O_F:O_F + BRANCH_WIDTH]).astype(BF16)
    for g in range(N_GROUPS):
        sl = slice(g * GROUP_DIM, (g + 1) * GROUP_DIM)
        z = _dot(f[:, sl], cs_ref[...])
        zr_ref[:, sl] = z[:, :GROUP_DIM].astype(BF16)
        zi_ref[:, sl] = z[:, GROUP_DIM:].astype(BF16)

    u = _gelu_tanh(_dot(h, win_ref[:, O_U:O_U + BRANCH_WIDTH]))
    v = _gelu_tanh(_dot(h, win_ref[:, O_V:O_V + BRANCH_WIDTH]))
    mu = jnp.mean(v, axis=-1, keepdims=True)
    vc = v - mu
    var = jnp.mean(vc * vc, axis=-1, keepdims=True)
    vn = (vc * lax.rsqrt(var + EPS) * lng_ref[...] + lnb_ref[...]).astype(BF16)
    rows = []
    for n in range(tm // CHUNK):
        cols = []
        for g in range(N_GROUPS):
            blk = vn[n * CHUNK:(n + 1) * CHUNK, g * GROUP_DIM:(g + 1) * GROUP_DIM]
            cols.append(_dot(ws_ref[g], blk))
        rows.append(jnp.concatenate(cols, axis=1) + bias_ref[...])
    mixed = jnp.concatenate(rows, axis=0)
    gated = (u * mixed).astype(BF16)
    y_b = _dot(gated, wso_ref[...])

    gate_b = _sigmoid(_dot(h, win_ref[:, O_GB:O_GB + D_MODEL]))
    pb_ref[...] = (gate_b * y_b).astype(BF16)
    gate_a = _sigmoid(_dot(h, win_ref[:, O_GA:O_GA + D_MODEL]))
    sga_ref[...] = gate_a.astype(BF16)


def _mix_in(x2d, g_pre, w_in, cs128, ln_g, ln_b, w_s, bias_full, w_so, *, tm):
    t = x2d.shape[0]
    tok = lambda w: pl.BlockSpec((tm, w), lambda i: (i, 0))
    return pl.pallas_call(
        functools.partial(_mix_in_kernel, tm=tm),
        grid=(t // tm,),
        in_specs=[tok(D_MODEL), _const_spec(g_pre.shape), _const_spec(w_in.shape),
                  _const_spec(cs128.shape), _const_spec(ln_g.shape), _const_spec(ln_b.shape),
                  _const_spec(w_s.shape), _const_spec(bias_full.shape), _const_spec(w_so.shape)],
        out_specs=[tok(BRANCH_WIDTH), tok(BRANCH_WIDTH), tok(D_MODEL), tok(D_MODEL)],
        out_shape=[jax.ShapeDtypeStruct((t, BRANCH_WIDTH), BF16),
                   jax.ShapeDtypeStruct((t, BRANCH_WIDTH), BF16),
                   jax.ShapeDtypeStruct((t, D_MODEL), BF16),
                   jax.ShapeDtypeStruct((t, D_MODEL), BF16)],
        compiler_params=pltpu.CompilerParams(dimension_semantics=("arbitrary",),
                                             vmem_limit_bytes=VMEM_LIMIT_V7X),
        name="mix_in",
    )(x2d, g_pre, w_in, cs128, ln_g, ln_b, w_s, bias_full, w_so)


def _cadd(a, b):
    return (a[0] + b[0], a[1] + b[1])


def _csub(a, b):
    return (a[0] - b[0], a[1] - b[1])


def _cmul_const(a, wr, wi):
    return (a[0] * wr - a[1] * wi, a[0] * wi + a[1] * wr)


def _dft4(x0, x1, x2, x3):
    t0, t1 = _cadd(x0, x2), _csub(x0, x2)
    t2, t3 = _cadd(x1, x3), _csub(x1, x3)
    mi_t3 = (t3[1], -t3[0])
    return _cadd(t0, t2), _cadd(t1, mi_t3), _csub(t0, t2), _csub(t1, mi_t3)


def _dft_blocks(xs):
    n = len(xs)
    if n == 4:
        return list(_dft4(*xs))
    assert n == 16
    inner = [_dft4(xs[n2], xs[4 + n2], xs[8 + n2], xs[12 + n2]) for n2 in range(4)]
    out = [None] * 16
    for k1 in range(4):
        tw = []
        for n2 in range(4):
            m = (n2 * k1) % 16
            val = inner[n2][k1]
            if m != 0:
                ang = -2.0 * math.pi * m / 16.0
                val = _cmul_const(val, math.cos(ang), math.sin(ang))
            tw.append(val)
        outer = _dft4(*tw)
        for k2 in range(4):
            out[k1 + 4 * k2] = outer[k2]
    return out


def _seq_dft_kernel(zr_ref, zi_ref, g_ref, o_ref, y_ref, *, n1, rb):
    k1 = pl.program_id(2)

    @pl.when(k1 == 0)
    def _():
        def body(r, carry):
            r0 = pl.multiple_of(r * rb, rb)
            xs = []
            for b in range(n1):
                rows = pl.ds(b * DFT_N2 + r0, rb)
                xs.append((zr_ref[rows, :].astype(F32), zi_ref[rows, :].astype(F32)))
            ys = _dft_blocks(xs)
            for b in range(n1):
                y_ref[b, pl.ds(r0, rb), :] = ys[b][0].astype(BF16)
                y_ref[b, pl.ds(DFT_N2 + r0, rb), :] = ys[b][1].astype(BF16)
            return carry
        lax.fori_loop(0, DFT_N2 // rb, body, 0)

    o_ref[...] = _dot(g_ref[k1], y_ref[k1]).astype(BF16)


def _seq_dft(zr, zi, gmat, *, nb, s, cw):
    n1 = s // DFT_N2
    nh = BRANCH_WIDTH // cw
    out = pl.pallas_call(
        functools.partial(_seq_dft_kernel, n1=n1, rb=16),
        grid=(nb, nh, n1),
        in_specs=[pl.BlockSpec((s, cw), lambda b, h, k: (b, h)),
                  pl.BlockSpec((s, cw), lambda b, h, k: (b, h)),
                  _const_spec(gmat.shape)],
        out_specs=pl.BlockSpec((None, DFT_N2, cw), lambda b, h, k: (b, 0, k * nh + h)),
        out_shape=jax.ShapeDtypeStruct((nb, DFT_N2, n1 * BRANCH_WIDTH), BF16),
        scratch_shapes=[pltpu.VMEM((n1, 2 * DFT_N2, cw), BF16)],
        compiler_params=pltpu.CompilerParams(
            dimension_semantics=("arbitrary", "arbitrary", "arbitrary"),
            vmem_limit_bytes=VMEM_LIMIT_V7X),
        name="seq_dft",
    )(zr, zi, gmat)
    return out.reshape(nb * s, BRANCH_WIDTH)


def _dft_stage_matrix(s):
    n1 = s // DFT_N2
    k = (jnp.arange(n1, dtype=jnp.int32)[:, None, None]
         + n1 * jnp.arange(DFT_N2, dtype=jnp.int32)[None, :, None])
    j = jnp.arange(DFT_N2, dtype=jnp.int32)[None, None, :]
    m = jnp.bitwise_and(j * k, s - 1).astype(F32)
    th = m * (2.0 * math.pi / s)
    scale = 1.0 / math.sqrt(s)
    return jnp.concatenate([jnp.cos(th) * scale, jnp.sin(th) * scale], axis=-1).astype(BF16)


def _channel_dft_matrix():
    c = jnp.arange(GROUP_DIM, dtype=jnp.int32)
    m = jnp.bitwise_and(c[:, None] * c[None, :], GROUP_DIM - 1).astype(F32)
    th = m * (2.0 * math.pi / GROUP_DIM)
    scale = 1.0 / math.sqrt(GROUP_DIM)
    return jnp.concatenate([jnp.cos(th) * scale, -jnp.sin(th) * scale], axis=-1).astype(BF16)


def _mix_out_kernel(x_ref, fm_ref, sga_ref, pb_ref, wfo_ref, wo_ref, g_ref, o_ref):
    y_a = _dot(fm_ref[...], wfo_ref[...])
    merged = sga_ref[...].astype(F32) * y_a + pb_ref[...].astype(F32)
    o = _dot(merged.astype(BF16), wo_ref[...])
    o_ref[...] = x_ref[...] + o * _rms_scale(o) * g_ref[...]


def _mix_out(x2d, fm, sga, pb, w_fo, w_o, g_post, *, tm):
    t = x2d.shape[0]
    tok = lambda w: pl.BlockSpec((tm, w), lambda i: (i, 0))
    return pl.pallas_call(
        _mix_out_kernel,
        grid=(t // tm,),
        in_specs=[tok(D_MODEL), tok(BRANCH_WIDTH), tok(D_MODEL), tok(D_MODEL),
                  _const_spec(w_fo.shape), _const_spec(w_o.shape), _const_spec(g_post.shape)],
        out_specs=tok(D_MODEL),
        out_shape=jax.ShapeDtypeStruct((t, D_MODEL), F32),
        compiler_params=pltpu.CompilerParams(dimension_semantics=("arbitrary",),
                                             vmem_limit_bytes=VMEM_LIMIT_V7X),
        name="mix_out",
    )(x2d, fm, sga, pb, w_fo, w_o, g_post)


HALO = 8


def _ffn_kernel(x_ref, xp_ref, xn_ref, g_ref, wup_ref, cw_ref, cb_ref, wdn_ref, gp_ref, o_ref,
                lhs_ref, acc_ref, *, tm, fc, tiles_per_seq):
    t_in_seq = pl.program_id(0) % tiles_per_seq
    keep_prev = (t_in_seq != 0).astype(F32)
    keep_next = (t_in_seq != tiles_per_seq - 1).astype(F32)

    x = x_ref[...]
    g = g_ref[...]
    lhs_ref[0:tm, :] = (x * _rms_scale(x) * g).astype(BF16)
    halo = jnp.concatenate([xp_ref[...], xn_ref[...]], axis=0)
    lhs_ref[tm:tm + 2 * HALO, :] = (halo * _rms_scale(halo) * g).astype(BF16)

    row = lax.broadcasted_iota(jnp.int32, (tm, 1), 0)
    first_row = row == 0
    last_row = row == tm - 1

    def conv(up_all, col0):
        up = up_all[0:tm]
        prev_row = up_all[tm + HALO - 1:tm + HALO] * keep_prev
        next_row = up_all[tm + HALO:tm + HALO + 1] * keep_next
        up_m = jnp.where(first_row, prev_row, pltpu.roll(up, 1, axis=0))
        up_p = jnp.where(last_row, next_row, pltpu.roll(up, tm - 1, axis=0))
        w = cw_ref[:, col0:col0 + fc]
        return (up_m * w[0:1] + up * w[1:2] + up_p * w[2:3]) + cb_ref[:, col0:col0 + fc]

    lhs = lhs_ref[...]
    for j in range(D_FF // fc):
        cg = conv(_dot(lhs, wup_ref[:, j * fc:(j + 1) * fc]), j * fc)
        cv = conv(_dot(lhs, wup_ref[:, D_FF + j * fc:D_FF + (j + 1) * fc]), D_FF + j * fc)
        act = (_gelu_tanh(cg) * cv).astype(BF16)
        part = _dot(act, wdn_ref[j * fc:(j + 1) * fc, :])
        if j == 0:
            acc_ref[...] = part
        else:
            acc_ref[...] += part

    ff = acc_ref[...]
    o_ref[...] = x + ff * _rms_scale(ff) * gp_ref[...]


def _ffn(x2d, g_pre, w_up, conv_w, conv_b, w_down, g_post, *, s, tm, fc):
    t = x2d.shape[0]
    tiles_per_seq = s // tm
    n_halo_blocks = t // HALO
    tok = pl.BlockSpec((tm, D_MODEL), lambda i: (i, 0))
    prev = pl.BlockSpec((HALO, D_MODEL), lambda i: (jnp.maximum(i * (tm // HALO) - 1, 0), 0))
    nxt = pl.BlockSpec((HALO, D_MODEL),
                       lambda i: (jnp.minimum((i + 1) * (tm // HALO), n_halo_blocks - 1), 0))
    return pl.pallas_call(
        functools.partial(_ffn_kernel, tm=tm, fc=fc, tiles_per_seq=tiles_per_seq),
        grid=(t // tm,),
        in_specs=[tok, prev, nxt, _const_spec(g_pre.shape), _const_spec(w_up.shape),
                  _const_spec(conv_w.shape), _const_spec(conv_b.shape),
                  _const_spec(w_down.shape), _const_spec(g_post.shape)],
        out_specs=tok,
        out_shape=jax.ShapeDtypeStruct((t, D_MODEL), F32),
        scratch_shapes=[pltpu.VMEM((tm + 2 * HALO, D_MODEL), BF16),
                        pltpu.VMEM((tm, D_MODEL), F32)],
        compiler_params=pltpu.CompilerParams(dimension_semantics=("arbitrary",),
                                             vmem_limit_bytes=VMEM_LIMIT_V7X),
        name="ffn",
    )(x2d, x2d, x2d, g_pre, w_up, conv_w, conv_b, w_down, g_post)


def _layer(x, p, consts):
    nb, s, d = x.shape
    x2d = x.reshape(nb * s, d)
    zr, zi, sga, pb = _mix_in(x2d, p["g_pre_mix"], p["w_in"], consts["cs128"], p["ln_g"],
                              p["ln_b"], p["w_s"], p["bias_full"], p["w_so"], tm=512)
    fm = _seq_dft(zr, zi, consts["gmat"][s], nb=nb, s=s, cw=256)
    x1 = _mix_out(x2d, fm, sga, pb, p["w_fo"], p["w_o"], p["g_post_mix"], tm=512)
    y = _ffn(x1, p["g_pre_ffn"], p["w_up"], p["conv_w"], p["conv_b"], p["w_down"],
             p["g_post_ffn"], s=s, tm=512, fc=512)
    return y.reshape(nb, s, d)


def kernel(x_prompt, x_sample, norm_pre_mix, w_in, sgu_ln_g, sgu_ln_b, sgu_w_s, sgu_b_s,
           w_fourier_out, w_sgu_out, w_o, norm_post_mix, norm_pre_ffn,
           w_up, conv_w, conv_b, w_down, norm_post_ffn):
    depth = w_in.shape[0]
    consts = {
        "cs128": _channel_dft_matrix(),
        "gmat": {s: _dft_stage_matrix(s) for s in {x_prompt.shape[1], x_sample.shape[1]}},
    }
    y_prompt, y_sample = x_prompt, x_sample
    for l in range(depth):
        row = lambda a: a[l].reshape(1, -1).astype(F32)
        p = {
            "g_pre_mix": row(norm_pre_mix), "w_in": w_in[l].astype(BF16),
            "ln_g": row(sgu_ln_g), "ln_b": row(sgu_ln_b),
            "w_s": sgu_w_s[l].astype(BF16),
            "bias_full": jnp.repeat(sgu_b_s[l].T.astype(F32), GROUP_DIM, axis=1),
            "w_so": w_sgu_out[l].astype(BF16), "w_fo": w_fourier_out[l].astype(BF16),
            "w_o": w_o[l].astype(BF16), "g_post_mix": row(norm_post_mix),
            "g_pre_ffn": row(norm_pre_ffn), "w_up": w_up[l].astype(BF16),
            "conv_w": conv_w[l].astype(F32), "conv_b": row(conv_b),
            "w_down": w_down[l].astype(BF16), "g_post_ffn": row(norm_post_ffn),
        }
        y_prompt = _layer(y_prompt, p, consts)
        y_sample = _layer(y_sample, p, consts)
    return (y_prompt, y_sample)
```

```python
import functools
import math

import jax
import jax.numpy as jnp
from jax import lax
from jax.experimental import pallas as pl
from jax.experimental.pallas import tpu as pltpu

D_MODEL = 1024
N_GROUPS = 4
GROUP_DIM = 128
BRANCH_WIDTH = N_GROUPS * GROUP_DIM
CHUNK = 128
D_FF = 4 * D_MODEL
EPS = 1e-6

O_F, O_U, O_V, O_GA, O_GB = 0, 512, 1024, 1536, 2560

SUBLANES = 8
LANES = 128
DFT_N2 = 512
VMEM_LIMIT_V7X = 56 * 1024 * 1024

BF16 = jnp.bfloat16
F32 = jnp.float32


def _const_spec(shape):
    nd = len(shape)
    return pl.BlockSpec(shape, lambda *_: (0,) * nd, pipeline_mode=pl.Buffered(1))


def _rms_scale(x):
    return lax.rsqrt(jnp.mean(x * x, axis=-1, keepdims=True) + EPS)


def _gelu_tanh(x):
    k = math.sqrt(2.0 / math.pi)
    inner = x * (k + (k * 0.044715) * (x * x))
    return 0.5 * x * (1.0 + jnp.tanh(inner))


def _sigmoid(x):
    return 1.0 / (1.0 + jnp.exp(-x))


def _dot(a, b):
    return jnp.dot(a, b, preferred_element_type=F32)


def _mix_in_kernel(x_ref, g_ref, win_ref, cs_ref, lng_ref, lnb_ref, ws_ref, bias_ref, wso_ref,
                   zr_ref, zi_ref, sga_ref, pb_ref, *, tm):
    x = x_ref[...]
    h = (x * _rms_scale(x) * g_ref[...]).astype(BF16)

    f = _dot(h, win_ref[:, O_F:O_F + BRANCH_WIDTH]).astype(BF16)
    for g in range(N_GROUPS):
        sl = slice(g * GROUP_DIM, (g + 1) * GROUP_DIM)
        z = _dot(f[:, sl], cs_ref[...])
        zr_ref[:, sl] = z[:, :GROUP_DIM].astype(BF16)
        zi_ref[:, sl] = z[:, GROUP_DIM:].astype(BF16)

    u = _gelu_tanh(_dot(h, win_ref[:, O_U:O_U + BRANCH_WIDTH]))
    v = _gelu_tanh(_dot(h, win_ref[:, O_V:O_V + BRANCH_WIDTH]))
    mu = jnp.mean(v, axis=-1, keepdims=True)
    vc = v - mu
    var = jnp.mean(vc * vc, axis=-1, keepdims=True)
    vn = (vc * lax.rsqrt(var + EPS) * lng_ref[...] + lnb_ref[...]).astype(BF16)
    rows = []
    for n in range(tm // CHUNK):
        cols = []
        for g in range(N_GROUPS):
            blk = vn[n * CHUNK:(n + 1) * CHUNK, g * GROUP_DIM:(g + 1) * GROUP_DIM]
            cols.append(_dot(ws_ref[g], blk))
        rows.append(jnp.concatenate(cols, axis=1) + bias_ref[...])
    mixed = jnp.concatenate(rows, axis=0)
    gated = (u * mixed).astype(BF16)
    y_b = _dot(gated, wso_ref[...])

    gate_b = _sigmoid(_dot(h, win_ref[:, O_GB:O_GB + D_MODEL]))
    pb_ref[...] = (gate_b * y_b).astype(BF16)
    gate_a = _sigmoid(_dot(h, win_ref[:, O_GA:O_GA + D_MODEL]))
    sga_ref[...] = gate_a.astype(BF16)


def _mix_in(x2d, g_pre, w_in, cs128, ln_g, ln_b, w_s, bias_full, w_so, *, tm):
    t = x2d.shape[0]
    tok = lambda w: pl.BlockSpec((tm, w), lambda i: (i, 0))
    return pl.pallas_call(
        functools.partial(_mix_in_kernel, tm=tm),
        grid=(t // tm,),
        in_specs=[tok(D_MODEL), _const_spec(g_pre.shape), _const_spec(w_in.shape),
                  _const_spec(cs128.shape), _const_spec(ln_g.shape), _const_spec(ln_b.shape),
                  _const_spec(w_s.shape), _const_spec(bias_full.shape), _const_spec(w_so.shape)],
        out_specs=[tok(BRANCH_WIDTH), tok(BRANCH_WIDTH), tok(D_MODEL), tok(D_MODEL)],
        out_shape=[jax.ShapeDtypeStruct((t, BRANCH_WIDTH), BF16),
                   jax.ShapeDtypeStruct((t, BRANCH_WIDTH), BF16),
                   jax.ShapeDtypeStruct((t, D_MODEL), BF16),
                   jax.ShapeDtypeStruct((t, D_MODEL), BF16)],
        compiler_params=pltpu.CompilerParams(dimension_semantics=("arbitrary",),
                                             vmem_limit_bytes=VMEM_LIMIT_V7X),
        name="mix_in",
    )(x2d, g_pre, w_in, cs128, ln_g, ln_b, w_s, bias_full, w_so)


def _cadd(a, b):
    return (a[0] + b[0], a[1] + b[1])


def _csub(a, b):
    return (a[0] - b[0], a[1] - b[1])


def _cmul_const(a, wr, wi):
    return (a[0] * wr - a[1] * wi, a[0] * wi + a[1] * wr)


def _dft4(x0, x1, x2, x3):
    t0, t1 = _cadd(x0, x2), _csub(x0, x2)
    t2, t3 = _cadd(x1, x3), _csub(x1, x3)
    mi_t3 = (t3[1], -t3[0])
    return _cadd(t0, t2), _cadd(t1, mi_t3), _csub(t0, t2), _csub(t1, mi_t3)


def _dft_blocks(xs):
    n = len(xs)
    if n == 4:
        return list(_dft4(*xs))
    assert n == 16
    inner = [_dft4(xs[n2], xs[4 + n2], xs[8 + n2], xs[12 + n2]) for n2 in range(4)]
    out = [None] * 16
    for k1 in range(4):
        tw = []
        for n2 in range(4):
            m = (n2 * k1) % 16
            val = inner[n2][k1]
            if m != 0:
                ang = -2.0 * math.pi * m / 16.0
                val = _cmul_const(val, math.cos(ang), math.sin(ang))
            tw.append(val)
        outer = _dft4(*tw)
        for k2 in range(4):
            out[k1 + 4 * k2] = outer[k2]
    return out


def _seq_dft_kernel(zr_ref, zi_ref, g_ref, o_ref, y_ref, *, n1, rb):
    k1 = pl.program_id(2)

    @pl.when(k1 == 0)
    def _():
        def body(r, carry):
            r0 = pl.multiple_of(r * rb, rb)
            xs = []
            for b in range(n1):
                rows = pl.ds(b * DFT_N2 + r0, rb)
                xs.append((zr_ref[rows, :].astype(F32), zi_ref[rows, :].astype(F32)))
            ys = _dft_blocks(xs)
            for b in range(n1):
                y_ref[b, pl.ds(r0, rb), :] = ys[b][0].astype(BF16)
                y_ref[b, pl.ds(DFT_N2 + r0, rb), :] = ys[b][1].astype(BF16)
            return carry
        lax.fori_loop(0, DFT_N2 // rb, body, 0)

    o_ref[...] = _dot(g_ref[k1], y_ref[k1]).astype(BF16)


def _seq_dft(zr, zi, gmat, *, nb, s, cw):
    n1 = s // DFT_N2
    nh = BRANCH_WIDTH // cw
    out = pl.pallas_call(
        functools.partial(_seq_dft_kernel, n1=n1, rb=16),
        grid=(nb, nh, n1),
        in_specs=[pl.BlockSpec((s, cw), lambda b, h, k: (b, h)),
                  pl.BlockSpec((s, cw), lambda b, h, k: (b, h)),
                  _const_spec(gmat.shape)],
        out_specs=pl.BlockSpec((None, DFT_N2, cw), lambda b, h, k: (b, 0, k * nh + h)),
        out_shape=jax.ShapeDtypeStruct((nb, DFT_N2, n1 * BRANCH_WIDTH), BF16),
        scratch_shapes=[pltpu.VMEM((n1, 2 * DFT_N2, cw), BF16)],
        compiler_params=pltpu.CompilerParams(
            dimension_semantics=("arbitrary", "arbitrary", "arbitrary"),
            vmem_limit_bytes=VMEM_LIMIT_V7X),
        name="seq_dft",
    )(zr, zi, gmat)
    return out.reshape(nb * DFT_N2, n1 * BRANCH_WIDTH)


def _dft_stage_matrix(s):
    n1 = s // DFT_N2
    k = (jnp.arange(n1, dtype=jnp.int32)[:, None, None]
         + n1 * jnp.arange(DFT_N2, dtype=jnp.int32)[None, :, None])
    j = jnp.arange(DFT_N2, dtype=jnp.int32)[None, None, :]
    m = jnp.bitwise_and(j * k, s - 1).astype(F32)
    th = m * (2.0 * math.pi / s)
    scale = 1.0 / math.sqrt(s)
    return jnp.concatenate([jnp.cos(th) * scale, jnp.sin(th) * scale], axis=-1).astype(BF16)


def _channel_dft_matrix():
    c = jnp.arange(GROUP_DIM, dtype=jnp.int32)
    m = jnp.bitwise_and(c[:, None] * c[None, :], GROUP_DIM - 1).astype(F32)
    th = m * (2.0 * math.pi / GROUP_DIM)
    scale = 1.0 / math.sqrt(GROUP_DIM)
    return jnp.concatenate([jnp.cos(th) * scale, -jnp.sin(th) * scale], axis=-1).astype(BF16)


def _mix_out_kernel(x_ref, fm_ref, sga_ref, pb_ref, wfo_ref, wo_ref, g_ref, o_ref, fs_ref, *, n1):
    rows = fm_ref.shape[0]
    for k1 in range(n1):
        for c in range(BRANCH_WIDTH // LANES):
            col = k1 * BRANCH_WIDTH + c * LANES
            fs_ref[c, pl.ds(k1, rows, stride=n1), :] = fm_ref[:, col:col + LANES].astype(F32)
    fm = jnp.concatenate([fs_ref[c] for c in range(BRANCH_WIDTH // LANES)], axis=1).astype(BF16)
    y_a = _dot(fm, wfo_ref[...])
    merged = sga_ref[...].astype(F32) * y_a + pb_ref[...].astype(F32)
    o = _dot(merged.astype(BF16), wo_ref[...])
    o_ref[...] = x_ref[...] + o * _rms_scale(o) * g_ref[...]


def _mix_out(x2d, fm, sga, pb, w_fo, w_o, g_post, *, tm, n1):
    t = x2d.shape[0]
    tok = lambda w: pl.BlockSpec((tm, w), lambda i: (i, 0))
    return pl.pallas_call(
        functools.partial(_mix_out_kernel, n1=n1),
        grid=(t // tm,),
        in_specs=[tok(D_MODEL), pl.BlockSpec((tm // n1, n1 * BRANCH_WIDTH), lambda i: (i, 0)),
                  tok(D_MODEL), tok(D_MODEL),
                  _const_spec(w_fo.shape), _const_spec(w_o.shape), _const_spec(g_post.shape)],
        out_specs=tok(D_MODEL),
        out_shape=jax.ShapeDtypeStruct((t, D_MODEL), F32),
        scratch_shapes=[pltpu.VMEM((BRANCH_WIDTH // LANES, tm, LANES), F32)],
        compiler_params=pltpu.CompilerParams(dimension_semantics=("arbitrary",),
                                             vmem_limit_bytes=VMEM_LIMIT_V7X),
        name="mix_out",
    )(x2d, fm, sga, pb, w_fo, w_o, g_post)


HALO = 8
SEG_PAD = 8


def _ffn_kernel(x_ref, xp_ref, xn_ref, g_ref, wup_ref, cw_ref, cb_ref, wdn_ref, gp_ref, o_ref,
                lhs_ref, acc_ref, xs_ref, ys_ref, *, tm, fc, tiles_per_seq):
    seg = tm // SUBLANES
    pitch = seg + SEG_PAD
    n_lane_blocks = D_MODEL // LANES
    t_in_seq = pl.program_id(0) % tiles_per_seq
    keep_prev = (t_in_seq != 0).astype(F32)
    keep_next = (t_in_seq != tiles_per_seq - 1).astype(F32)

    for c in range(n_lane_blocks):
        for s in range(SUBLANES):
            xs_ref[c, s * pitch:s * pitch + seg, :] = x_ref[s * seg:(s + 1) * seg,
                                                           c * LANES:(c + 1) * LANES]
    xp = jnp.concatenate(
        [jnp.concatenate([xs_ref[c, pl.ds(v, SUBLANES, stride=pitch), :]
                          for c in range(n_lane_blocks)], axis=1)
         for v in range(seg)], axis=0)

    g = g_ref[...]
    lhs_ref[0:tm, :] = (xp * _rms_scale(xp) * g).astype(BF16)
    halo = jnp.concatenate([xp_ref[...], xn_ref[...]], axis=0)
    lhs_ref[tm:tm + 2 * HALO, :] = (halo * _rms_scale(halo) * g).astype(BF16)

    sub = lax.broadcasted_iota(jnp.int32, (SUBLANES, 1), 0)

    def conv(up_all, col0):
        up = up_all[0:tm]
        prev_row = up_all[tm + HALO - 1:tm + HALO] * keep_prev
        next_row = up_all[tm + HALO:tm + HALO + 1] * keep_next
        head = jnp.where(sub == 0, prev_row, pltpu.roll(up[tm - SUBLANES:tm], 1, axis=0))
        tail = jnp.where(sub == SUBLANES - 1, next_row,
                         pltpu.roll(up[0:SUBLANES], SUBLANES - 1, axis=0))
        up_m = jnp.concatenate([head, up[0:tm - SUBLANES]], axis=0)
        up_p = jnp.concatenate([up[SUBLANES:tm], tail], axis=0)
        w = cw_ref[:, col0:col0 + fc]
        return (up_m * w[0:1] + up * w[1:2] + up_p * w[2:3]) + cb_ref[:, col0:col0 + fc]

    lhs = lhs_ref[...]

    def up_pair(j):
        return (_dot(lhs, wup_ref[:, j * fc:(j + 1) * fc]),
                _dot(lhs, wup_ref[:, D_FF + j * fc:D_FF + (j + 1) * fc]))

    n_chunks = D_FF // fc
    nxt = up_pair(0)
    for j in range(n_chunks):
        ug, uv = nxt
        if j + 1 < n_chunks:
            nxt = up_pair(j + 1)
        act = (_gelu_tanh(conv(ug, j * fc)) * conv(uv, D_FF + j * fc)).astype(BF16)
        part = _dot(act, wdn_ref[j * fc:(j + 1) * fc, :])
        if j == 0:
            acc_ref[...] = part
        else:
            acc_ref[...] += part

    ff = acc_ref[...]
    normed = ff * _rms_scale(ff) * gp_ref[...]
    for v in range(seg):
        for c in range(n_lane_blocks):
            ys_ref[c, pl.ds(v, SUBLANES, stride=pitch), :] = normed[v * SUBLANES:(v + 1) * SUBLANES,
                                                                    c * LANES:(c + 1) * LANES]
    for c in range(n_lane_blocks):
        for s in range(SUBLANES):
            o_ref[s * seg:(s + 1) * seg, c * LANES:(c + 1) * LANES] = (
                x_ref[s * seg:(s + 1) * seg, c * LANES:(c + 1) * LANES]
                + ys_ref[c, s * pitch:s * pitch + seg, :])


def _ffn(x2d, g_pre, w_up, conv_w, conv_b, w_down, g_post, *, s, tm, fc):
    t = x2d.shape[0]
    tiles_per_seq = s // tm
    n_halo_blocks = t // HALO
    pitch = tm // SUBLANES + SEG_PAD
    perm_scratch = pltpu.VMEM((D_MODEL // LANES, SUBLANES * pitch, LANES), F32)
    tok = pl.BlockSpec((tm, D_MODEL), lambda i: (i, 0))
    prev = pl.BlockSpec((HALO, D_MODEL), lambda i: (jnp.maximum(i * (tm // HALO) - 1, 0), 0))
    nxt = pl.BlockSpec((HALO, D_MODEL),
                       lambda i: (jnp.minimum((i + 1) * (tm // HALO), n_halo_blocks - 1), 0))
    return pl.pallas_call(
        functools.partial(_ffn_kernel, tm=tm, fc=fc, tiles_per_seq=tiles_per_seq),
        grid=(t // tm,),
        in_specs=[tok, prev, nxt, _const_spec(g_pre.shape), _const_spec(w_up.shape),
                  _const_spec(conv_w.shape), _const_spec(conv_b.shape),
                  _const_spec(w_down.shape), _const_spec(g_post.shape)],
        out_specs=tok,
        out_shape=jax.ShapeDtypeStruct((t, D_MODEL), F32),
        scratch_shapes=[pltpu.VMEM((tm + 2 * HALO, D_MODEL), BF16),
                        pltpu.VMEM((tm, D_MODEL), F32), perm_scratch, perm_scratch],
        compiler_params=pltpu.CompilerParams(dimension_semantics=("arbitrary",),
                                             vmem_limit_bytes=VMEM_LIMIT_V7X),
        name="ffn",
    )(x2d, x2d, x2d, g_pre, w_up, conv_w, conv_b, w_down, g_post)


def _layer(x, p, consts):
    nb, s, d = x.shape
    x2d = x.reshape(nb * s, d)
    zr, zi, sga, pb = _mix_in(x2d, p["g_pre_mix"], p["w_in"], consts["cs128"], p["ln_g"],
                              p["ln_b"], p["w_s"], p["bias_full"], p["w_so"], tm=512)
    fm = _seq_dft(zr, zi, consts["gmat"][s], nb=nb, s=s, cw=256)
    x1 = _mix_out(x2d, fm, sga, pb, p["w_fo"], p["w_o"], p["g_post_mix"], tm=512,
                  n1=s // DFT_N2)
    y = _ffn(x1, p["g_pre_ffn"], p["w_up"], p["conv_w"], p["conv_b"], p["w_down"],
             p["g_post_ffn"], s=s, tm=512, fc=512)
    return y.reshape(nb, s, d)


def kernel(x_prompt, x_sample, norm_pre_mix, w_in, sgu_ln_g, sgu_ln_b, sgu_w_s, sgu_b_s,
           w_fourier_out, w_sgu_out, w_o, norm_post_mix, norm_pre_ffn,
           w_up, conv_w, conv_b, w_down, norm_post_ffn):
    depth = w_in.shape[0]
    consts = {
        "cs128": _channel_dft_matrix(),
        "gmat": {s: _dft_stage_matrix(s) for s in {x_prompt.shape[1], x_sample.shape[1]}},
    }
    y_prompt, y_sample = x_prompt, x_sample
    for l in range(depth):
        row = lambda a: a[l].reshape(1, -1).astype(F32)
        p = {
            "g_pre_mix": row(norm_pre_mix), "w_in": w_in[l].astype(BF16),
            "ln_g": row(sgu_ln_g), "ln_b": row(sgu_ln_b),
            "w_s": sgu_w_s[l].astype(BF16),
            "bias_full": jnp.repeat(sgu_b_s[l].T.astype(F32), GROUP_DIM, axis=1),
            "w_so": w_sgu_out[l].astype(BF16), "w_fo": w_fourier_out[l].astype(BF16),
            "w_o": w_o[l].astype(BF16), "g_post_mix": row(norm_post_mix),
            "g_pre_ffn": row(norm_pre_ffn), "w_up": w_up[l].astype(BF16),
            "conv_w": conv_w[l].astype(F32), "conv_b": row(conv_b),
            "w_down": w_down[l].astype(BF16), "g_post_ffn": row(norm_post_ffn),
        }
        y_prompt = _layer(y_prompt, p, consts)
        y_sample = _layer(y_sample, p, consts)
    return (y_prompt, y_sample)
```

```python
import functools
import math

import jax
import jax.numpy as jnp
from jax import lax
from jax.experimental import pallas as pl
from jax.experimental.pallas import tpu as pltpu

D_MODEL = 1024
N_GROUPS = 4
GROUP_DIM = 128
BRANCH_WIDTH = N_GROUPS * GROUP_DIM
CHUNK = 128
D_FF = 4 * D_MODEL
EPS = 1e-6

O_F, O_U, O_V, O_GA, O_GB = 0, 512, 1024, 1536, 2560

SUBLANES = 8
LANES = 128
DFT_N2 = 512
VMEM_LIMIT_V7X = 56 * 1024 * 1024

BF16 = jnp.bfloat16
F32 = jnp.float32


def _const_spec(shape):
    nd = len(shape)
    return pl.BlockSpec(shape, lambda *_: (0,) * nd, pipeline_mode=pl.Buffered(1))


def _rms_scale(x):
    return lax.rsqrt(jnp.mean(x * x, axis=-1, keepdims=True) + EPS)


def _gelu_inner(x):
    k = math.sqrt(2.0 / math.pi)
    return x * (k + (k * 0.044715) * (x * x))


def _gelu_tanh(x):
    return 0.5 * x * (1.0 + jnp.tanh(_gelu_inner(x)))


def _sigmoid(x):
    return 1.0 / (1.0 + jnp.exp(-x))


def _dot(a, b):
    return jnp.dot(a, b, preferred_element_type=F32)


def _mix_in_kernel(x_ref, g_ref, win_ref, cs_ref, lng_ref, lnb_ref, ws_ref, bias_ref, wso_ref,
                   zr_ref, zi_ref, sga_ref, pb_ref, *, tm):
    x = x_ref[...]
    h = (x * _rms_scale(x) * g_ref[...]).astype(BF16)

    f = _dot(h, win_ref[:, O_F:O_F + BRANCH_WIDTH]).astype(BF16)
    for g in range(N_GROUPS):
        sl = slice(g * GROUP_DIM, (g + 1) * GROUP_DIM)
        z = _dot(f[:, sl], cs_ref[...])
        zr_ref[:, sl] = z[:, :GROUP_DIM].astype(BF16)
        zi_ref[:, sl] = z[:, GROUP_DIM:].astype(BF16)

    u = _gelu_tanh(_dot(h, win_ref[:, O_U:O_U + BRANCH_WIDTH]))
    v = _gelu_tanh(_dot(h, win_ref[:, O_V:O_V + BRANCH_WIDTH]))
    mu = jnp.mean(v, axis=-1, keepdims=True)
    vc = v - mu
    var = jnp.mean(vc * vc, axis=-1, keepdims=True)
    vn = (vc * lax.rsqrt(var + EPS) * lng_ref[...] + lnb_ref[...]).astype(BF16)
    n_chunks = tm // CHUNK
    per_group = []
    for g in range(N_GROUPS):
        rhs = jnp.concatenate(
            [vn[n * CHUNK:(n + 1) * CHUNK, g * GROUP_DIM:(g + 1) * GROUP_DIM]
             for n in range(n_chunks)], axis=1)
        per_group.append(_dot(ws_ref[g], rhs))
    mixed = jnp.concatenate(
        [jnp.concatenate([per_group[g][:, n * GROUP_DIM:(n + 1) * GROUP_DIM]
                          for g in range(N_GROUPS)], axis=1) + bias_ref[...]
         for n in range(n_chunks)], axis=0)
    gated = (u * mixed).astype(BF16)
    y_b = _dot(gated, wso_ref[...])

    gate_b = _sigmoid(_dot(h, win_ref[:, O_GB:O_GB + D_MODEL]))
    pb_ref[...] = (gate_b * y_b).astype(BF16)
    gate_a = _sigmoid(_dot(h, win_ref[:, O_GA:O_GA + D_MODEL]))
    sga_ref[...] = gate_a.astype(BF16)


def _mix_in(x2d, g_pre, w_in, cs128, ln_g, ln_b, w_s, bias_full, w_so, *, tm):
    t = x2d.shape[0]
    tok = lambda w: pl.BlockSpec((tm, w), lambda i: (i, 0))
    return pl.pallas_call(
        functools.partial(_mix_in_kernel, tm=tm),
        grid=(t // tm,),
        in_specs=[tok(D_MODEL), _const_spec(g_pre.shape), _const_spec(w_in.shape),
                  _const_spec(cs128.shape), _const_spec(ln_g.shape), _const_spec(ln_b.shape),
                  _const_spec(w_s.shape), _const_spec(bias_full.shape), _const_spec(w_so.shape)],
        out_specs=[tok(BRANCH_WIDTH), tok(BRANCH_WIDTH), tok(D_MODEL), tok(D_MODEL)],
        out_shape=[jax.ShapeDtypeStruct((t, BRANCH_WIDTH), BF16),
                   jax.ShapeDtypeStruct((t, BRANCH_WIDTH), BF16),
                   jax.ShapeDtypeStruct((t, D_MODEL), BF16),
                   jax.ShapeDtypeStruct((t, D_MODEL), BF16)],
        compiler_params=pltpu.CompilerParams(dimension_semantics=("arbitrary",),
                                             vmem_limit_bytes=VMEM_LIMIT_V7X),
        name="mix_in",
    )(x2d, g_pre, w_in, cs128, ln_g, ln_b, w_s, bias_full, w_so)


def _cadd(a, b):
    return (a[0] + b[0], a[1] + b[1])


def _csub(a, b):
    return (a[0] - b[0], a[1] - b[1])


def _cmul_const(a, wr, wi):
    return (a[0] * wr - a[1] * wi, a[0] * wi + a[1] * wr)


def _dft4(x0, x1, x2, x3):
    t0, t1 = _cadd(x0, x2), _csub(x0, x2)
    t2, t3 = _cadd(x1, x3), _csub(x1, x3)
    mi_t3 = (t3[1], -t3[0])
    return _cadd(t0, t2), _cadd(t1, mi_t3), _csub(t0, t2), _csub(t1, mi_t3)


def _dft_blocks(xs):
    n = len(xs)
    if n == 4:
        return list(_dft4(*xs))
    assert n == 16
    inner = [_dft4(xs[n2], xs[4 + n2], xs[8 + n2], xs[12 + n2]) for n2 in range(4)]
    out = [None] * 16
    for k1 in range(4):
        tw = []
        for n2 in range(4):
            m = (n2 * k1) % 16
            val = inner[n2][k1]
            if m != 0:
                ang = -2.0 * math.pi * m / 16.0
                val = _cmul_const(val, math.cos(ang), math.sin(ang))
            tw.append(val)
        outer = _dft4(*tw)
        for k2 in range(4):
            out[k1 + 4 * k2] = outer[k2]
    return out


def _seq_dft_kernel(zr_ref, zi_ref, g_ref, o_ref, y_ref, *, n1, rb, cw):
    def body(r, carry):
        r0 = pl.multiple_of(r * rb, rb)
        xs = []
        for b in range(n1):
            rows = pl.ds(b * DFT_N2 + r0, rb)
            xs.append((zr_ref[rows, :].astype(F32), zi_ref[rows, :].astype(F32)))
        ys = _dft_blocks(xs)
        for b in range(n1):
            y_ref[b, pl.ds(r0, rb), :] = ys[b][0].astype(BF16)
            y_ref[b, pl.ds(DFT_N2 + r0, rb), :] = ys[b][1].astype(BF16)
        return carry
    lax.fori_loop(0, DFT_N2 // rb, body, 0)

    for k1 in range(n1):
        o_ref[:, k1 * cw:(k1 + 1) * cw] = _dot(g_ref[k1], y_ref[k1]).astype(BF16)


def _seq_dft(zr, zi, gmat, *, nb, s, cw):
    n1 = s // DFT_N2
    nh = BRANCH_WIDTH // cw
    return pl.pallas_call(
        functools.partial(_seq_dft_kernel, n1=n1, rb=16, cw=cw),
        grid=(nb, nh),
        in_specs=[pl.BlockSpec((s, cw), lambda b, h: (b, h)),
                  pl.BlockSpec((s, cw), lambda b, h: (b, h)),
                  _const_spec(gmat.shape)],
        out_specs=pl.BlockSpec((None, None, DFT_N2, n1 * cw), lambda b, h: (b, h, 0, 0)),
        out_shape=jax.ShapeDtypeStruct((nb, nh, DFT_N2, n1 * cw), BF16),
        scratch_shapes=[pltpu.VMEM((n1, 2 * DFT_N2, cw), BF16)],
        compiler_params=pltpu.CompilerParams(
            dimension_semantics=("arbitrary", "arbitrary"),
            vmem_limit_bytes=VMEM_LIMIT_V7X),
        name="seq_dft",
    )(zr, zi, gmat)


def _dft_stage_matrix(s):
    n1 = s // DFT_N2
    j = jnp.arange(DFT_N2, dtype=jnp.int32)
    ma = jnp.bitwise_and(jnp.arange(n1, dtype=jnp.int32)[:, None] * j[None, :], s - 1)
    a = ma.astype(F32) * (2.0 * math.pi / s)
    mb = jnp.bitwise_and(j[:, None] * j[None, :], DFT_N2 - 1)
    b = mb.astype(F32) * (2.0 * math.pi / DFT_N2)
    scale = 1.0 / math.sqrt(s)
    ca, sa = (jnp.cos(a) * scale)[:, None, :], (jnp.sin(a) * scale)[:, None, :]
    cb, sb = jnp.cos(b)[None], jnp.sin(b)[None]
    return jnp.concatenate([ca * cb - sa * sb, sa * cb + ca * sb], axis=-1).astype(BF16)


def _channel_dft_matrix():
    c = jnp.arange(GROUP_DIM, dtype=jnp.int32)
    m = jnp.bitwise_and(c[:, None] * c[None, :], GROUP_DIM - 1).astype(F32)
    th = m * (2.0 * math.pi / GROUP_DIM)
    scale = 1.0 / math.sqrt(GROUP_DIM)
    return jnp.concatenate([jnp.cos(th) * scale, -jnp.sin(th) * scale], axis=-1).astype(BF16)


def _unpermute_dft(fm_ref, fs_ref, n1):
    nh, rows, width = fm_ref.shape
    cw = width // n1
    for h in range(nh):
        for k1 in range(n1):
            for c in range(cw // LANES):
                col = k1 * cw + c * LANES
                fs_ref[h * (cw // LANES) + c, pl.ds(k1, rows, stride=n1), :] = (
                    fm_ref[h, :, col:col + LANES].astype(F32))
    return jnp.concatenate([fs_ref[c] for c in range(BRANCH_WIDTH // LANES)], axis=1).astype(BF16)


def _mix_out_kernel(x_ref, fm_ref, sga_ref, pb_ref, wfo_ref, wo_ref, g_ref, o_ref, fs_ref, *, n1):
    fm = _unpermute_dft(fm_ref, fs_ref, n1)
    y_a = _dot(fm, wfo_ref[...])
    merged = sga_ref[...].astype(F32) * y_a + pb_ref[...].astype(F32)
    o = _dot(merged.astype(BF16), wo_ref[...])
    o_ref[...] = x_ref[...] + o * _rms_scale(o) * g_ref[...]


def _mix_out(x2d, fm, sga, pb, w_fo, w_o, g_post, *, tm, n1):
    t = x2d.shape[0]
    _, nh, _, width = fm.shape
    tiles_per_seq = n1 * DFT_N2 // tm
    tok = lambda w: pl.BlockSpec((tm, w), lambda i: (i, 0))
    fm_spec = pl.BlockSpec((None, nh, tm // n1, width),
                           lambda i: (i // tiles_per_seq, 0, i % tiles_per_seq, 0))
    return pl.pallas_call(
        functools.partial(_mix_out_kernel, n1=n1),
        grid=(t // tm,),
        in_specs=[tok(D_MODEL), fm_spec, tok(D_MODEL), tok(D_MODEL),
                  _const_spec(w_fo.shape), _const_spec(w_o.shape), _const_spec(g_post.shape)],
        out_specs=tok(D_MODEL),
        out_shape=jax.ShapeDtypeStruct((t, D_MODEL), F32),
        scratch_shapes=[pltpu.VMEM((BRANCH_WIDTH // LANES, tm, LANES), F32)],
        compiler_params=pltpu.CompilerParams(dimension_semantics=("arbitrary",),
                                             vmem_limit_bytes=VMEM_LIMIT_V7X),
        name="mix_out",
    )(x2d, fm, sga, pb, w_fo, w_o, g_post)


HALO = 8
SEG_PAD = 8


def _ffn_prologue(x_ref, xprev_ref, xnext_ref, g_ref, lhs_ref, xs_ref, *, tm):
    seg = tm // SUBLANES
    pitch = seg + SEG_PAD
    n_lane_blocks = D_MODEL // LANES
    for c in range(n_lane_blocks):
        for s in range(SUBLANES):
            xs_ref[c, s * pitch:s * pitch + seg, :] = x_ref[s * seg:(s + 1) * seg,
                                                           c * LANES:(c + 1) * LANES]
    xp = jnp.concatenate(
        [jnp.concatenate([xs_ref[c, pl.ds(v, SUBLANES, stride=pitch), :]
                          for c in range(n_lane_blocks)], axis=1)
         for v in range(seg)], axis=0)
    g = g_ref[...]
    lhs_ref[0:tm, :] = (xp * _rms_scale(xp) * g).astype(BF16)
    halo = jnp.concatenate([xprev_ref[...], xnext_ref[...]], axis=0)
    lhs_ref[tm:tm + 2 * HALO, :] = (halo * _rms_scale(halo) * g).astype(BF16)


def _ffn_epilogue(x_ref, acc_ref, gp_ref, o_ref, ys_ref, *, tm):
    seg = tm // SUBLANES
    pitch = seg + SEG_PAD
    n_lane_blocks = D_MODEL // LANES
    ff = acc_ref[...]
    normed = ff * _rms_scale(ff) * gp_ref[...]
    for v in range(seg):
        for c in range(n_lane_blocks):
            ys_ref[c, pl.ds(v, SUBLANES, stride=pitch), :] = normed[v * SUBLANES:(v + 1) * SUBLANES,
                                                                    c * LANES:(c + 1) * LANES]
    for c in range(n_lane_blocks):
        for s in range(SUBLANES):
            o_ref[s * seg:(s + 1) * seg, c * LANES:(c + 1) * LANES] = (
                x_ref[s * seg:(s + 1) * seg, c * LANES:(c + 1) * LANES]
                + ys_ref[c, s * pitch:s * pitch + seg, :])


def _ffn_kernel(x_ref, xprev_ref, xnext_ref, g_ref, wup_ref, cw_ref, cb_ref, wdn_ref,
                gp_ref, o_ref, lhs_ref, acc_ref, xs_ref, ys_ref,
                *, tm, fc, tiles_per_seq, lookahead, down_group):
    _ffn_prologue(x_ref, xprev_ref, xnext_ref, g_ref, lhs_ref, xs_ref, tm=tm)
    _ffn_matmuls(lhs_ref, wup_ref, cw_ref, cb_ref, wdn_ref, acc_ref,
                 t_in_seq=pl.program_id(0) % tiles_per_seq, tm=tm, fc=fc,
                 tiles_per_seq=tiles_per_seq, lookahead=lookahead, down_group=down_group)
    _ffn_epilogue(x_ref, acc_ref, gp_ref, o_ref, ys_ref, tm=tm)


def _ffn_matmuls(lhs_ref, wup_ref, cw_ref, cb_ref, wdn_ref, acc_ref,
                 *, t_in_seq, tm, fc, tiles_per_seq, lookahead, down_group):
    keep_prev = (t_in_seq != 0).astype(F32)
    keep_next = (t_in_seq != tiles_per_seq - 1).astype(F32)
    sub = lax.broadcasted_iota(jnp.int32, (SUBLANES, 1), 0)

    def conv(up_all, col0, gain=1.0):
        up = up_all[0:tm]
        prev_row = up_all[tm + HALO - 1:tm + HALO] * keep_prev
        next_row = up_all[tm + HALO:tm + HALO + 1] * keep_next
        head = jnp.where(sub == 0, prev_row, pltpu.roll(up[tm - SUBLANES:tm], 1, axis=0))
        tail = jnp.where(sub == SUBLANES - 1, next_row,
                         pltpu.roll(up[0:SUBLANES], SUBLANES - 1, axis=0))
        up_m = jnp.concatenate([head, up[0:tm - SUBLANES]], axis=0)
        up_p = jnp.concatenate([up[SUBLANES:tm], tail], axis=0)
        w = cw_ref[:, col0:col0 + fc] * gain
        return (up_m * w[0:1] + up * w[1:2] + up_p * w[2:3]) + cb_ref[:, col0:col0 + fc] * gain

    lhs = lhs_ref[...]

    def up_pair(j):
        return (_dot(lhs, wup_ref[:, j * fc:(j + 1) * fc]),
                _dot(lhs, wup_ref[:, D_FF + j * fc:D_FF + (j + 1) * fc]))

    n_chunks = D_FF // fc
    pending = [up_pair(j) for j in range(min(lookahead, n_chunks))]
    acts = []
    for j in range(n_chunks):
        ug, uv = pending.pop(0)
        if j + lookahead < n_chunks:
            pending.append(up_pair(j + lookahead))
        cg = conv(ug, j * fc)
        half_cv = conv(uv, D_FF + j * fc, gain=0.5)
        acts.append((cg * (1.0 + jnp.tanh(_gelu_inner(cg))) * half_cv).astype(BF16))
        if len(acts) == down_group:
            j0 = j + 1 - down_group
            part = _dot(jnp.concatenate(acts, axis=1), wdn_ref[j0 * fc:(j + 1) * fc, :])
            acts = []
            if j0 == 0:
                acc_ref[...] = part
            else:
                acc_ref[...] += part


def _ffn(x2d, g_pre, w_up, conv_w, conv_b, w_down, g_post, *, s, tm, fc, lookahead, down_group):
    t = x2d.shape[0]
    tiles_per_seq = s // tm
    n_halo_blocks = t // HALO
    halo_per_tile = tm // HALO
    pitch = tm // SUBLANES + SEG_PAD
    perm_scratch = pltpu.VMEM((D_MODEL // LANES, SUBLANES * pitch, LANES), F32)
    tok = pl.BlockSpec((tm, D_MODEL), lambda i: (i, 0))
    prev = pl.BlockSpec((HALO, D_MODEL), lambda i: (jnp.maximum(i * halo_per_tile - 1, 0), 0))
    nxt = pl.BlockSpec((HALO, D_MODEL),
                       lambda i: (jnp.minimum((i + 1) * halo_per_tile, n_halo_blocks - 1), 0))
    return pl.pallas_call(
        functools.partial(_ffn_kernel, tm=tm, fc=fc, tiles_per_seq=tiles_per_seq,
                          lookahead=lookahead, down_group=down_group),
        grid=(t // tm,),
        in_specs=[tok, prev, nxt, _const_spec(g_pre.shape), _const_spec(w_up.shape),
                  _const_spec(conv_w.shape), _const_spec(conv_b.shape),
                  _const_spec(w_down.shape), _const_spec(g_post.shape)],
        out_specs=tok,
        out_shape=jax.ShapeDtypeStruct((t, D_MODEL), F32),
        scratch_shapes=[pltpu.VMEM((tm + 2 * HALO, D_MODEL), BF16),
                        pltpu.VMEM((tm, D_MODEL), F32), perm_scratch, perm_scratch],
        compiler_params=pltpu.CompilerParams(dimension_semantics=("arbitrary",),
                                             vmem_limit_bytes=VMEM_LIMIT_V7X),
        name="ffn",
    )(x2d, x2d, x2d, g_pre, w_up, conv_w, conv_b, w_down, g_post)


def _layer(x, p, consts):
    nb, s, d = x.shape
    x2d = x.reshape(nb * s, d)
    zr, zi, sga, pb = _mix_in(x2d, p["g_pre_mix"], p["w_in"], consts["cs128"], p["ln_g"],
                              p["ln_b"], p["w_s"], p["bias_full"], p["w_so"], tm=512)
    fm = _seq_dft(zr, zi, consts["gmat"][s], nb=nb, s=s, cw=256)
    x1 = _mix_out(x2d, fm, sga, pb, p["w_fo"], p["w_o"], p["g_post_mix"], tm=512,
                  n1=s // DFT_N2)
    y = _ffn(x1, p["g_pre_ffn"], p["w_up"], p["conv_w"], p["conv_b"], p["w_down"],
             p["g_post_ffn"], s=s, tm=512, fc=512, lookahead=3, down_group=2)
    return y.reshape(nb, s, d)


def kernel(x_prompt, x_sample, norm_pre_mix, w_in, sgu_ln_g, sgu_ln_b, sgu_w_s, sgu_b_s,
           w_fourier_out, w_sgu_out, w_o, norm_post_mix, norm_pre_ffn,
           w_up, conv_w, conv_b, w_down, norm_post_ffn):
    depth = w_in.shape[0]
    consts = {
        "cs128": _channel_dft_matrix(),
        "gmat": {s: _dft_stage_matrix(s) for s in {x_prompt.shape[1], x_sample.shape[1]}},
    }
    y_prompt, y_sample = x_prompt, x_sample
    for l in range(depth):
        row = lambda a: a[l].reshape(1, -1).astype(F32)
        p = {
            "g_pre_mix": row(norm_pre_mix), "w_in": w_in[l].astype(BF16),
            "ln_g": row(sgu_ln_g), "ln_b": row(sgu_ln_b),
            "w_s": sgu_w_s[l].astype(BF16),
            "bias_full": jnp.repeat(sgu_b_s[l].T.astype(F32), GROUP_DIM, axis=1),
            "w_so": w_sgu_out[l].astype(BF16), "w_fo": w_fourier_out[l].astype(BF16),
            "w_o": w_o[l].astype(BF16), "g_post_mix": row(norm_post_mix),
            "g_pre_ffn": row(norm_pre_ffn), "w_up": w_up[l].astype(BF16),
            "conv_w": conv_w[l].astype(F32), "conv_b": row(conv_b),
            "w_down": w_down[l].astype(BF16), "g_post_ffn": row(norm_post_ffn),
        }
        y_prompt = _layer(y_prompt, p, consts)
        y_sample = _layer(y_sample, p, consts)
    return (y_prompt, y_sample)
```

```python
import functools
import math

import jax
import jax.numpy as jnp
from jax import lax
from jax.experimental import pallas as pl
from jax.experimental.pallas import tpu as pltpu

D_MODEL = 1024
N_GROUPS = 4
GROUP_DIM = 128
BRANCH_WIDTH = N_GROUPS * GROUP_DIM
CHUNK = 128
D_FF = 4 * D_MODEL
EPS = 1e-6

O_F, O_U, O_V, O_GA, O_GB = 0, 512, 1024, 1536, 2560

SUBLANES = 8
LANES = 128
DFT_N2 = 512
VMEM_LIMIT_V7X = 56 * 1024 * 1024

BF16 = jnp.bfloat16
F32 = jnp.float32


def _const_spec(shape):
    nd = len(shape)
    return pl.BlockSpec(shape, lambda *_: (0,) * nd, pipeline_mode=pl.Buffered(1))


def _rms_scale(x):
    return lax.rsqrt(jnp.mean(x * x, axis=-1, keepdims=True) + EPS)


def _gelu_inner(x):
    k = math.sqrt(2.0 / math.pi)
    return x * (k + (k * 0.044715) * (x * x))


def _gelu_tanh(x):
    return 0.5 * x * (1.0 + jnp.tanh(_gelu_inner(x)))


def _sigmoid(x):
    return 1.0 / (1.0 + jnp.exp(-x))


def _dot(a, b):
    return jnp.dot(a, b, preferred_element_type=F32)


def _mix_in_kernel(x_ref, g_ref, win_ref, cs_ref, lng_ref, lnb_ref, ws_ref, bias_ref, wso_ref,
                   zr_ref, zi_ref, sga_ref, pb_ref, *, tm):
    x = x_ref[...]
    h = (x * _rms_scale(x) * g_ref[...]).astype(BF16)

    f = _dot(h, win_ref[:, O_F:O_F + BRANCH_WIDTH]).astype(BF16)
    for g in range(N_GROUPS):
        sl = slice(g * GROUP_DIM, (g + 1) * GROUP_DIM)
        z = _dot(f[:, sl], cs_ref[...])
        zr_ref[:, sl] = z[:, :GROUP_DIM].astype(BF16)
        zi_ref[:, sl] = z[:, GROUP_DIM:].astype(BF16)

    u = _gelu_tanh(_dot(h, win_ref[:, O_U:O_U + BRANCH_WIDTH]))
    v = _gelu_tanh(_dot(h, win_ref[:, O_V:O_V + BRANCH_WIDTH]))
    mu = jnp.mean(v, axis=-1, keepdims=True)
    vc = v - mu
    var = jnp.mean(vc * vc, axis=-1, keepdims=True)
    vn = (vc * lax.rsqrt(var + EPS) * lng_ref[...] + lnb_ref[...]).astype(BF16)
    n_chunks = tm // CHUNK
    per_group = []
    for g in range(N_GROUPS):
        rhs = jnp.concatenate(
            [vn[n * CHUNK:(n + 1) * CHUNK, g * GROUP_DIM:(g + 1) * GROUP_DIM]
             for n in range(n_chunks)], axis=1)
        per_group.append(_dot(ws_ref[g], rhs))
    mixed = jnp.concatenate(
        [jnp.concatenate([per_group[g][:, n * GROUP_DIM:(n + 1) * GROUP_DIM]
                          for g in range(N_GROUPS)], axis=1) + bias_ref[...]
         for n in range(n_chunks)], axis=0)
    gated = (u * mixed).astype(BF16)
    y_b = _dot(gated, wso_ref[...])

    gate_b = _sigmoid(_dot(h, win_ref[:, O_GB:O_GB + D_MODEL]))
    pb_ref[...] = (gate_b * y_b).astype(BF16)
    gate_a = _sigmoid(_dot(h, win_ref[:, O_GA:O_GA + D_MODEL]))
    sga_ref[...] = gate_a.astype(BF16)


def _mix_in(x2d, g_pre, w_in, cs128, ln_g, ln_b, w_s, bias_full, w_so, *, tm):
    t = x2d.shape[0]
    tok = lambda w: pl.BlockSpec((tm, w), lambda i: (i, 0))
    return pl.pallas_call(
        functools.partial(_mix_in_kernel, tm=tm),
        grid=(t // tm,),
        in_specs=[tok(D_MODEL), _const_spec(g_pre.shape), _const_spec(w_in.shape),
                  _const_spec(cs128.shape), _const_spec(ln_g.shape), _const_spec(ln_b.shape),
                  _const_spec(w_s.shape), _const_spec(bias_full.shape), _const_spec(w_so.shape)],
        out_specs=[tok(BRANCH_WIDTH), tok(BRANCH_WIDTH), tok(D_MODEL), tok(D_MODEL)],
        out_shape=[jax.ShapeDtypeStruct((t, BRANCH_WIDTH), BF16),
                   jax.ShapeDtypeStruct((t, BRANCH_WIDTH), BF16),
                   jax.ShapeDtypeStruct((t, D_MODEL), BF16),
                   jax.ShapeDtypeStruct((t, D_MODEL), BF16)],
        compiler_params=pltpu.CompilerParams(dimension_semantics=("arbitrary",),
                                             vmem_limit_bytes=VMEM_LIMIT_V7X),
        name="mix_in",
    )(x2d, g_pre, w_in, cs128, ln_g, ln_b, w_s, bias_full, w_so)


def _cadd(a, b):
    return (a[0] + b[0], a[1] + b[1])


def _csub(a, b):
    return (a[0] - b[0], a[1] - b[1])


def _cmul_const(a, wr, wi):
    return (a[0] * wr - a[1] * wi, a[0] * wi + a[1] * wr)


def _dft4(x0, x1, x2, x3):
    t0, t1 = _cadd(x0, x2), _csub(x0, x2)
    t2, t3 = _cadd(x1, x3), _csub(x1, x3)
    mi_t3 = (t3[1], -t3[0])
    return _cadd(t0, t2), _cadd(t1, mi_t3), _csub(t0, t2), _csub(t1, mi_t3)


def _dft_blocks(xs):
    n = len(xs)
    if n == 4:
        return list(_dft4(*xs))
    assert n == 16
    inner = [_dft4(xs[n2], xs[4 + n2], xs[8 + n2], xs[12 + n2]) for n2 in range(4)]
    out = [None] * 16
    for k1 in range(4):
        tw = []
        for n2 in range(4):
            m = (n2 * k1) % 16
            val = inner[n2][k1]
            if m != 0:
                ang = -2.0 * math.pi * m / 16.0
                val = _cmul_const(val, math.cos(ang), math.sin(ang))
            tw.append(val)
        outer = _dft4(*tw)
        for k2 in range(4):
            out[k1 + 4 * k2] = outer[k2]
    return out


def _seq_dft_kernel(zr_ref, zi_ref, g_ref, o_ref, y_ref, *, n1, rb, cw):
    def body(r, carry):
        r0 = pl.multiple_of(r * rb, rb)
        xs = []
        for b in range(n1):
            rows = pl.ds(b * DFT_N2 + r0, rb)
            xs.append((zr_ref[rows, :].astype(F32), zi_ref[rows, :].astype(F32)))
        ys = _dft_blocks(xs)
        for b in range(n1):
            y_ref[b, pl.ds(r0, rb), :] = ys[b][0].astype(BF16)
            y_ref[b, pl.ds(DFT_N2 + r0, rb), :] = ys[b][1].astype(BF16)
        return carry
    lax.fori_loop(0, DFT_N2 // rb, body, 0)

    for k1 in range(n1):
        o_ref[:, k1 * cw:(k1 + 1) * cw] = _dot(g_ref[k1], y_ref[k1]).astype(BF16)


def _seq_dft(zr, zi, gmat, *, nb, s, cw):
    n1 = s // DFT_N2
    nh = BRANCH_WIDTH // cw
    return pl.pallas_call(
        functools.partial(_seq_dft_kernel, n1=n1, rb=16, cw=cw),
        grid=(nb, nh),
        in_specs=[pl.BlockSpec((s, cw), lambda b, h: (b, h)),
                  pl.BlockSpec((s, cw), lambda b, h: (b, h)),
                  _const_spec(gmat.shape)],
        out_specs=pl.BlockSpec((None, None, DFT_N2, n1 * cw), lambda b, h: (b, h, 0, 0)),
        out_shape=jax.ShapeDtypeStruct((nb, nh, DFT_N2, n1 * cw), BF16),
        scratch_shapes=[pltpu.VMEM((n1, 2 * DFT_N2, cw), BF16)],
        compiler_params=pltpu.CompilerParams(
            dimension_semantics=("arbitrary", "arbitrary"),
            vmem_limit_bytes=VMEM_LIMIT_V7X),
        name="seq_dft",
    )(zr, zi, gmat)


def _dft_stage_matrix(s):
    n1 = s // DFT_N2
    j = jnp.arange(DFT_N2, dtype=jnp.int32)
    ma = jnp.bitwise_and(jnp.arange(n1, dtype=jnp.int32)[:, None] * j[None, :], s - 1)
    a = ma.astype(F32) * (2.0 * math.pi / s)
    mb = jnp.bitwise_and(j[:, None] * j[None, :], DFT_N2 - 1)
    b = mb.astype(F32) * (2.0 * math.pi / DFT_N2)
    scale = 1.0 / math.sqrt(s)
    ca, sa = (jnp.cos(a) * scale)[:, None, :], (jnp.sin(a) * scale)[:, None, :]
    cb, sb = jnp.cos(b)[None], jnp.sin(b)[None]
    return jnp.concatenate([ca * cb - sa * sb, sa * cb + ca * sb], axis=-1).astype(BF16)


def _channel_dft_matrix():
    c = jnp.arange(GROUP_DIM, dtype=jnp.int32)
    m = jnp.bitwise_and(c[:, None] * c[None, :], GROUP_DIM - 1).astype(F32)
    th = m * (2.0 * math.pi / GROUP_DIM)
    scale = 1.0 / math.sqrt(GROUP_DIM)
    return jnp.concatenate([jnp.cos(th) * scale, -jnp.sin(th) * scale], axis=-1).astype(BF16)


def _unpermute_dft(fm_ref, fs_ref, n1):
    nh, rows, width = fm_ref.shape
    cw = width // n1
    for h in range(nh):
        for k1 in range(n1):
            for c in range(cw // LANES):
                col = k1 * cw + c * LANES
                fs_ref[h * (cw // LANES) + c, pl.ds(k1, rows, stride=n1), :] = (
                    fm_ref[h, :, col:col + LANES].astype(F32))
    return jnp.concatenate([fs_ref[c] for c in range(BRANCH_WIDTH // LANES)], axis=1).astype(BF16)


def _mix_out_kernel(fm_ref, sga_ref, pb_ref, wfo_ref, wo_ref, g_ref, o_ref, fs_ref, *, n1):
    fm = _unpermute_dft(fm_ref, fs_ref, n1)
    y_a = _dot(fm, wfo_ref[...])
    merged = sga_ref[...].astype(F32) * y_a + pb_ref[...].astype(F32)
    o = _dot(merged.astype(BF16), wo_ref[...])
    o_ref[...] = (o * _rms_scale(o) * g_ref[...]).astype(BF16)


def _mix_out(fm, sga, pb, w_fo, w_o, g_post, *, tm, n1):
    t = sga.shape[0]
    _, nh, _, width = fm.shape
    tiles_per_seq = n1 * DFT_N2 // tm
    tok = lambda w: pl.BlockSpec((tm, w), lambda i: (i, 0))
    fm_spec = pl.BlockSpec((None, nh, tm // n1, width),
                           lambda i: (i // tiles_per_seq, 0, i % tiles_per_seq, 0))
    return pl.pallas_call(
        functools.partial(_mix_out_kernel, n1=n1),
        grid=(t // tm,),
        in_specs=[fm_spec, tok(D_MODEL), tok(D_MODEL),
                  _const_spec(w_fo.shape), _const_spec(w_o.shape), _const_spec(g_post.shape)],
        out_specs=tok(D_MODEL),
        out_shape=jax.ShapeDtypeStruct((t, D_MODEL), BF16),
        scratch_shapes=[pltpu.VMEM((BRANCH_WIDTH // LANES, tm, LANES), F32)],
        compiler_params=pltpu.CompilerParams(dimension_semantics=("arbitrary",),
                                             vmem_limit_bytes=VMEM_LIMIT_V7X),
        name="mix_out",
    )(fm, sga, pb, w_fo, w_o, g_post)


HALO = 8
M_HALO = 16
SEG_PAD = 8


def _ffn_prologue(x_ref, xprev_ref, xnext_ref, m_ref, mprev_ref, mnext_ref, g_ref,
                  lhs_ref, xs_ref, *, tm):
    seg = tm // SUBLANES
    pitch = seg + SEG_PAD
    n_lane_blocks = D_MODEL // LANES
    x1 = x_ref[...] + m_ref[...].astype(F32)
    for c in range(n_lane_blocks):
        for s in range(SUBLANES):
            xs_ref[c, s * pitch:s * pitch + seg, :] = x1[s * seg:(s + 1) * seg,
                                                        c * LANES:(c + 1) * LANES]
    xp = jnp.concatenate(
        [jnp.concatenate([xs_ref[c, pl.ds(v, SUBLANES, stride=pitch), :]
                          for c in range(n_lane_blocks)], axis=1)
         for v in range(seg)], axis=0)
    g = g_ref[...]
    lhs_ref[0:tm, :] = (xp * _rms_scale(xp) * g).astype(BF16)
    mh = jnp.concatenate([mprev_ref[...].astype(F32)[M_HALO - HALO:M_HALO],
                          mnext_ref[...].astype(F32)[0:HALO]], axis=0)
    xh = jnp.concatenate([xprev_ref[...], xnext_ref[...]], axis=0)
    x1h = xh + mh
    lhs_ref[tm:tm + 2 * HALO, :] = (x1h * _rms_scale(x1h) * g).astype(BF16)


def _ffn_epilogue(ff, xs_ref, gp_ref, o_ref, ys_ref, sc_ref, *, tm):
    seg = tm // SUBLANES
    pitch = seg + SEG_PAD
    n_lane_blocks = D_MODEL // LANES
    scale = _rms_scale(ff)
    for v in range(seg):
        rows = slice(v * SUBLANES, (v + 1) * SUBLANES)
        for c in range(n_lane_blocks):
            ys_ref[c, pl.ds(v, SUBLANES, stride=pitch), :] = ff[rows, c * LANES:(c + 1) * LANES]
        sc_ref[pl.ds(v, SUBLANES, stride=pitch), :] = jnp.broadcast_to(scale[rows],
                                                                       (SUBLANES, LANES))
    g = gp_ref[...]
    for c in range(n_lane_blocks):
        for s in range(SUBLANES):
            rows = slice(s * pitch, s * pitch + seg)
            o_ref[s * seg:(s + 1) * seg, c * LANES:(c + 1) * LANES] = (
                xs_ref[c, rows, :]
                + ys_ref[c, rows, :] * sc_ref[rows, :] * g[:, c * LANES:(c + 1) * LANES])


def _ffn_kernel(x_ref, xprev_ref, xnext_ref, m_ref, mprev_ref, mnext_ref, g_ref,
                wup_ref, cw_ref, cb_ref, wdn_ref, gp_ref, o_ref, lhs_ref, acc_ref, xs_ref, ys_ref,
                sc_ref, *, tm, fc, tiles_per_seq, lookahead, down_group):
    _ffn_prologue(x_ref, xprev_ref, xnext_ref, m_ref, mprev_ref, mnext_ref, g_ref,
                  lhs_ref, xs_ref, tm=tm)
    ff = _ffn_matmuls(lhs_ref, wup_ref, cw_ref, cb_ref, wdn_ref, acc_ref,
                      t_in_seq=pl.program_id(0) % tiles_per_seq, tm=tm, fc=fc,
                      tiles_per_seq=tiles_per_seq, lookahead=lookahead, down_group=down_group)
    _ffn_epilogue(ff, xs_ref, gp_ref, o_ref, ys_ref, sc_ref, tm=tm)


def _ffn_matmuls(lhs_ref, wup_ref, cw_ref, cb_ref, wdn_ref, acc_ref,
                 *, t_in_seq, tm, fc, tiles_per_seq, lookahead, down_group):
    keep_prev = (t_in_seq != 0).astype(F32)
    keep_next = (t_in_seq != tiles_per_seq - 1).astype(F32)
    sub = lax.broadcasted_iota(jnp.int32, (SUBLANES, 1), 0)

    def conv(up_all, col0, gain=1.0):
        up = up_all[0:tm]
        prev_row = up_all[tm + HALO - 1:tm + HALO] * keep_prev
        next_row = up_all[tm + HALO:tm + HALO + 1] * keep_next
        head = jnp.where(sub == 0, prev_row, pltpu.roll(up[tm - SUBLANES:tm], 1, axis=0))
        tail = jnp.where(sub == SUBLANES - 1, next_row,
                         pltpu.roll(up[0:SUBLANES], SUBLANES - 1, axis=0))
        up_m = jnp.concatenate([head, up[0:tm - SUBLANES]], axis=0)
        up_p = jnp.concatenate([up[SUBLANES:tm], tail], axis=0)
        w = cw_ref[:, col0:col0 + fc] * gain
        return (up_m * w[0:1] + up * w[1:2] + up_p * w[2:3]) + cb_ref[:, col0:col0 + fc] * gain

    lhs = lhs_ref[...]

    def up_pair(j):
        return (_dot(lhs, wup_ref[:, j * fc:(j + 1) * fc]),
                _dot(lhs, wup_ref[:, D_FF + j * fc:D_FF + (j + 1) * fc]))

    n_chunks = D_FF // fc
    pending = [up_pair(j) for j in range(min(lookahead, n_chunks))]
    acts = []
    for j in range(n_chunks):
        ug, uv = pending.pop(0)
        if j + lookahead < n_chunks:
            pending.append(up_pair(j + lookahead))
        cg = conv(ug, j * fc)
        half_cv = conv(uv, D_FF + j * fc, gain=0.5)
        acts.append((cg * (1.0 + jnp.tanh(_gelu_inner(cg))) * half_cv).astype(BF16))
        if len(acts) == down_group:
            j0 = j + 1 - down_group
            part = _dot(jnp.concatenate(acts, axis=1), wdn_ref[j0 * fc:(j + 1) * fc, :])
            acts = []
            if j + 1 == n_chunks:
                return part if j0 == 0 else acc_ref[...] + part
            if j0 == 0:
                acc_ref[...] = part
            else:
                acc_ref[...] += part


def _ffn(x2d, m2d, g_pre, w_up, conv_w, conv_b, w_down, g_post,
         *, s, tm, fc, lookahead, down_group):
    t = x2d.shape[0]
    tiles_per_seq = s // tm
    pitch = tm // SUBLANES + SEG_PAD
    perm_scratch = pltpu.VMEM((D_MODEL // LANES, SUBLANES * pitch, LANES), F32)
    tok = pl.BlockSpec((tm, D_MODEL), lambda i: (i, 0))

    def halo_specs(rows):
        per_tile, n_blocks = tm // rows, t // rows
        return (pl.BlockSpec((rows, D_MODEL), lambda i: (jnp.maximum(i * per_tile - 1, 0), 0)),
                pl.BlockSpec((rows, D_MODEL),
                             lambda i: (jnp.minimum((i + 1) * per_tile, n_blocks - 1), 0)))

    x_prev, x_next = halo_specs(HALO)
    m_prev, m_next = halo_specs(M_HALO)
    return pl.pallas_call(
        functools.partial(_ffn_kernel, tm=tm, fc=fc, tiles_per_seq=tiles_per_seq,
                          lookahead=lookahead, down_group=down_group),
        grid=(t // tm,),
        in_specs=[tok, x_prev, x_next, tok, m_prev, m_next, _const_spec(g_pre.shape), _const_spec(w_up.shape),
                  _const_spec(conv_w.shape), _const_spec(conv_b.shape),
                  _const_spec(w_down.shape), _const_spec(g_post.shape)],
        out_specs=tok,
        out_shape=jax.ShapeDtypeStruct((t, D_MODEL), F32),
        scratch_shapes=[pltpu.VMEM((tm + 2 * HALO, D_MODEL), BF16),
                        pltpu.VMEM((tm, D_MODEL), F32), perm_scratch, perm_scratch,
                        pltpu.VMEM((SUBLANES * pitch, LANES), F32)],
        compiler_params=pltpu.CompilerParams(dimension_semantics=("arbitrary",),
                                             vmem_limit_bytes=VMEM_LIMIT_V7X),
        name="ffn",
    )(x2d, x2d, x2d, m2d, m2d, m2d, g_pre, w_up, conv_w, conv_b, w_down, g_post)


def _layer(x, p, consts):
    nb, s, d = x.shape
    x2d = x.reshape(nb * s, d)
    zr, zi, sga, pb = _mix_in(x2d, p["g_pre_mix"], p["w_in"], consts["cs128"], p["ln_g"],
                              p["ln_b"], p["w_s"], p["bias_full"], p["w_so"], tm=512)
    fm = _seq_dft(zr, zi, consts["gmat"][s], nb=nb, s=s, cw=256)
    m = _mix_out(fm, sga, pb, p["w_fo"], p["w_o"], p["g_post_mix"], tm=512, n1=s // DFT_N2)
    y = _ffn(x2d, m, p["g_pre_ffn"], p["w_up"], p["conv_w"], p["conv_b"],
             p["w_down"], p["g_post_ffn"], s=s, tm=512, fc=512, lookahead=3, down_group=2)
    return y.reshape(nb, s, d)


def kernel(x_prompt, x_sample, norm_pre_mix, w_in, sgu_ln_g, sgu_ln_b, sgu_w_s, sgu_b_s,
           w_fourier_out, w_sgu_out, w_o, norm_post_mix, norm_pre_ffn,
           w_up, conv_w, conv_b, w_down, norm_post_ffn):
    depth = w_in.shape[0]
    consts = {
        "cs128": _channel_dft_matrix(),
        "gmat": {s: _dft_stage_matrix(s) for s in {x_prompt.shape[1], x_sample.shape[1]}},
    }
    y_prompt, y_sample = x_prompt, x_sample
    for l in range(depth):
        row = lambda a: a[l].reshape(1, -1).astype(F32)
        p = {
            "g_pre_mix": row(norm_pre_mix), "w_in": w_in[l].astype(BF16),
            "ln_g": row(sgu_ln_g), "ln_b": row(sgu_ln_b),
            "w_s": sgu_w_s[l].astype(BF16),
            "bias_full": jnp.repeat(sgu_b_s[l].T.astype(F32), GROUP_DIM, axis=1),
            "w_so": w_sgu_out[l].astype(BF16), "w_fo": w_fourier_out[l].astype(BF16),
            "w_o": w_o[l].astype(BF16), "g_post_mix": row(norm_post_mix),
            "g_pre_ffn": row(norm_pre_ffn), "w_up": w_up[l].astype(BF16),
            "conv_w": conv_w[l].astype(F32), "conv_b": row(conv_b),
            "w_down": w_down[l].astype(BF16), "g_post_ffn": row(norm_post_ffn),
        }
        y_prompt = _layer(y_prompt, p, consts)
        y_sample = _layer(y_sample, p, consts)
    return (y_prompt, y_sample)
```

```python
import functools
import math

import jax
import jax.numpy as jnp
from jax import lax
from jax.experimental import pallas as pl
from jax.experimental.pallas import tpu as pltpu

D_MODEL = 1024
N_GROUPS = 4
GROUP_DIM = 128
BRANCH_WIDTH = N_GROUPS * GROUP_DIM
CHUNK = 128
D_FF = 4 * D_MODEL
EPS = 1e-6

O_F, O_U, O_V, O_GA, O_GB = 0, 512, 1024, 1536, 2560

SUBLANES = 8
LANES = 128
DFT_N2 = 512
VMEM_LIMIT_V7X = 56 * 1024 * 1024
VMEM_LIMIT_FFN_V7X = 62 * 1024 * 1024

BF16 = jnp.bfloat16
F32 = jnp.float32


def _const_spec(shape):
    nd = len(shape)
    return pl.BlockSpec(shape, lambda *_: (0,) * nd, pipeline_mode=pl.Buffered(1))


def _rms_scale(x):
    return lax.rsqrt(jnp.mean(x * x, axis=-1, keepdims=True) + EPS)


def _gelu_inner(x):
    k = math.sqrt(2.0 / math.pi)
    return x * (k + (k * 0.044715) * (x * x))


def _gelu_tanh(x):
    return 0.5 * x * (1.0 + jnp.tanh(_gelu_inner(x)))


def _sigmoid(x):
    return 1.0 / (1.0 + jnp.exp(-x))


def _dot(a, b):
    return jnp.dot(a, b, preferred_element_type=F32)


def _mix_in_kernel(x_ref, g_ref, win_ref, cs_ref, lng_ref, lnb_ref, ws_ref, bias_ref, wso_ref,
                   zr_ref, zi_ref, sga_ref, pb_ref, *, sub):
    for a in range(x_ref.shape[0] // sub):
        rows = pl.ds(a * sub, sub)
        _mix_in_tile(x_ref.at[rows], g_ref, win_ref, cs_ref, lng_ref, lnb_ref, ws_ref, bias_ref,
                     wso_ref, zr_ref.at[rows], zi_ref.at[rows], sga_ref.at[rows], pb_ref.at[rows],
                     tm=sub)


def _mix_in_tile(x_ref, g_ref, win_ref, cs_ref, lng_ref, lnb_ref, ws_ref, bias_ref, wso_ref,
                 zr_ref, zi_ref, sga_ref, pb_ref, *, tm):
    x = x_ref[...]
    h = (x * _rms_scale(x) * g_ref[...]).astype(BF16)

    v_pre = _dot(h, win_ref[:, O_V:O_V + BRANCH_WIDTH])
    u_pre = _dot(h, win_ref[:, O_U:O_U + BRANCH_WIDTH])
    f = _dot(h, win_ref[:, O_F:O_F + BRANCH_WIDTH]).astype(BF16)
    gate_a_pre = _dot(h, win_ref[:, O_GA:O_GA + D_MODEL])

    v = _gelu_tanh(v_pre)
    mu = jnp.mean(v, axis=-1, keepdims=True)
    vc = v - mu
    var = jnp.mean(vc * vc, axis=-1, keepdims=True)
    vn = (vc * lax.rsqrt(var + EPS) * lng_ref[...] + lnb_ref[...]).astype(BF16)

    for g in range(N_GROUPS):
        sl = slice(g * GROUP_DIM, (g + 1) * GROUP_DIM)
        z = _dot(f[:, sl], cs_ref[...])
        zr_ref[:, sl] = z[:, :GROUP_DIM].astype(BF16)
        zi_ref[:, sl] = z[:, GROUP_DIM:].astype(BF16)
    sga_ref[...] = _sigmoid(gate_a_pre).astype(BF16)

    n_chunks = tm // CHUNK
    per_group = []
    for g in range(N_GROUPS):
        rhs = jnp.concatenate(
            [vn[n * CHUNK:(n + 1) * CHUNK, g * GROUP_DIM:(g + 1) * GROUP_DIM]
             for n in range(n_chunks)], axis=1)
        per_group.append(_dot(ws_ref[g], rhs))
    gate_b_pre = _dot(h, win_ref[:, O_GB:O_GB + D_MODEL])
    mixed = jnp.concatenate(
        [jnp.concatenate([per_group[g][:, n * GROUP_DIM:(n + 1) * GROUP_DIM]
                          for g in range(N_GROUPS)], axis=1) + bias_ref[...]
         for n in range(n_chunks)], axis=0)
    gated = (_gelu_tanh(u_pre) * mixed).astype(BF16)
    y_b = _dot(gated, wso_ref[...])
    pb_ref[...] = (_sigmoid(gate_b_pre) * y_b).astype(BF16)


def _mix_in(x2d, g_pre, w_in, cs128, ln_g, ln_b, w_s, bias_full, w_so, *, tm, sub):
    t = x2d.shape[0]
    tok = lambda w: pl.BlockSpec((tm, w), lambda i: (i, 0))
    return pl.pallas_call(
        functools.partial(_mix_in_kernel, sub=sub),
        grid=(t // tm,),
        in_specs=[tok(D_MODEL), _const_spec(g_pre.shape), _const_spec(w_in.shape),
                  _const_spec(cs128.shape), _const_spec(ln_g.shape), _const_spec(ln_b.shape),
                  _const_spec(w_s.shape), _const_spec(bias_full.shape), _const_spec(w_so.shape)],
        out_specs=[tok(BRANCH_WIDTH), tok(BRANCH_WIDTH), tok(D_MODEL), tok(D_MODEL)],
        out_shape=[jax.ShapeDtypeStruct((t, BRANCH_WIDTH), BF16),
                   jax.ShapeDtypeStruct((t, BRANCH_WIDTH), BF16),
                   jax.ShapeDtypeStruct((t, D_MODEL), BF16),
                   jax.ShapeDtypeStruct((t, D_MODEL), BF16)],
        compiler_params=pltpu.CompilerParams(dimension_semantics=("arbitrary",),
                                             vmem_limit_bytes=VMEM_LIMIT_V7X),
        name="mix_in",
    )(x2d, g_pre, w_in, cs128, ln_g, ln_b, w_s, bias_full, w_so)


def _cadd(a, b):
    return (a[0] + b[0], a[1] + b[1])


def _csub(a, b):
    return (a[0] - b[0], a[1] - b[1])


def _cmul_const(a, wr, wi):
    return (a[0] * wr - a[1] * wi, a[0] * wi + a[1] * wr)


def _dft4(x0, x1, x2, x3):
    t0, t1 = _cadd(x0, x2), _csub(x0, x2)
    t2, t3 = _cadd(x1, x3), _csub(x1, x3)
    mi_t3 = (t3[1], -t3[0])
    return _cadd(t0, t2), _cadd(t1, mi_t3), _csub(t0, t2), _csub(t1, mi_t3)


def _dft_blocks(xs):
    n = len(xs)
    if n == 4:
        return list(_dft4(*xs))
    assert n == 16
    inner = [_dft4(xs[n2], xs[4 + n2], xs[8 + n2], xs[12 + n2]) for n2 in range(4)]
    out = [None] * 16
    for k1 in range(4):
        tw = []
        for n2 in range(4):
            m = (n2 * k1) % 16
            val = inner[n2][k1]
            if m != 0:
                ang = -2.0 * math.pi * m / 16.0
                val = _cmul_const(val, math.cos(ang), math.sin(ang))
            tw.append(val)
        outer = _dft4(*tw)
        for k2 in range(4):
            out[k1 + 4 * k2] = outer[k2]
    return out


def _seq_dft_kernel(zr_ref, zi_ref, g_ref, o_ref, y_ref, *, n1, rb, cw):
    def body(r, carry):
        r0 = pl.multiple_of(r * rb, rb)
        xs = []
        for b in range(n1):
            rows = pl.ds(b * DFT_N2 + r0, rb)
            xs.append((zr_ref[rows, :].astype(F32), zi_ref[rows, :].astype(F32)))
        ys = _dft_blocks(xs)
        for b in range(n1):
            y_ref[b, pl.ds(r0, rb), :] = ys[b][0].astype(BF16)
            y_ref[b, pl.ds(DFT_N2 + r0, rb), :] = ys[b][1].astype(BF16)
        return carry
    lax.fori_loop(0, DFT_N2 // rb, body, 0)

    for k1 in range(n1):
        o_ref[:, k1 * cw:(k1 + 1) * cw] = _dot(g_ref[k1], y_ref[k1]).astype(BF16)


def _seq_dft(zr, zi, gmat, *, nb, s, cw):
    n1 = s // DFT_N2
    nh = BRANCH_WIDTH // cw
    return pl.pallas_call(
        functools.partial(_seq_dft_kernel, n1=n1, rb=16, cw=cw),
        grid=(nb, nh),
        in_specs=[pl.BlockSpec((s, cw), lambda b, h: (b, h)),
                  pl.BlockSpec((s, cw), lambda b, h: (b, h)),
                  _const_spec(gmat.shape)],
        out_specs=pl.BlockSpec((None, None, DFT_N2, n1 * cw), lambda b, h: (b, h, 0, 0)),
        out_shape=jax.ShapeDtypeStruct((nb, nh, DFT_N2, n1 * cw), BF16),
        scratch_shapes=[pltpu.VMEM((n1, 2 * DFT_N2, cw), BF16)],
        compiler_params=pltpu.CompilerParams(
            dimension_semantics=("arbitrary", "arbitrary"),
            vmem_limit_bytes=VMEM_LIMIT_V7X),
        name="seq_dft",
    )(zr, zi, gmat)


def _dft_stage_matrix(s):
    n1 = s // DFT_N2
    j = jnp.arange(DFT_N2, dtype=jnp.int32)
    ma = jnp.bitwise_and(jnp.arange(n1, dtype=jnp.int32)[:, None] * j[None, :], s - 1)
    a = ma.astype(F32) * (2.0 * math.pi / s)
    mb = jnp.bitwise_and(j[:, None] * j[None, :], DFT_N2 - 1)
    b = mb.astype(F32) * (2.0 * math.pi / DFT_N2)
    scale = 1.0 / math.sqrt(s)
    ca, sa = (jnp.cos(a) * scale)[:, None, :], (jnp.sin(a) * scale)[:, None, :]
    cb, sb = jnp.cos(b)[None], jnp.sin(b)[None]
    return jnp.concatenate([ca * cb - sa * sb, sa * cb + ca * sb], axis=-1).astype(BF16)


def _channel_dft_matrix():
    c = jnp.arange(GROUP_DIM, dtype=jnp.int32)
    m = jnp.bitwise_and(c[:, None] * c[None, :], GROUP_DIM - 1).astype(F32)
    th = m * (2.0 * math.pi / GROUP_DIM)
    scale = 1.0 / math.sqrt(GROUP_DIM)
    return jnp.concatenate([jnp.cos(th) * scale, -jnp.sin(th) * scale], axis=-1).astype(BF16)


def _unpermute_dft(fm_ref, fs_ref, n1):
    nh, rows, width = fm_ref.shape
    cw = width // n1
    for h in range(nh):
        for k1 in range(n1):
            for c in range(cw // LANES):
                col = k1 * cw + c * LANES
                fs_ref[h * (cw // LANES) + c, pl.ds(k1, rows, stride=n1), :] = (
                    fm_ref[h, :, col:col + LANES].astype(F32))
    return jnp.concatenate([fs_ref[c] for c in range(BRANCH_WIDTH // LANES)], axis=1).astype(BF16)


def _mix_out_kernel(fm_ref, sga_ref, pb_ref, wfo_ref, wo_ref, g_ref, o_ref, fs_ref, *, n1, sub):
    fm = _unpermute_dft(fm_ref, fs_ref, n1)
    for a in range(fm.shape[0] // sub):
        rows = slice(a * sub, (a + 1) * sub)
        y_a = _dot(fm[rows], wfo_ref[...])
        merged = sga_ref[rows, :].astype(F32) * y_a + pb_ref[rows, :].astype(F32)
        o = _dot(merged.astype(BF16), wo_ref[...])
        o_ref[rows, :] = (o * _rms_scale(o) * g_ref[...]).astype(BF16)


def _mix_out(fm, sga, pb, w_fo, w_o, g_post, *, tm, sub, n1):
    t = sga.shape[0]
    _, nh, _, width = fm.shape
    tiles_per_seq = n1 * DFT_N2 // tm
    tok = lambda w: pl.BlockSpec((tm, w), lambda i: (i, 0))
    fm_spec = pl.BlockSpec((None, nh, tm // n1, width),
                           lambda i: (i // tiles_per_seq, 0, i % tiles_per_seq, 0))
    return pl.pallas_call(
        functools.partial(_mix_out_kernel, n1=n1, sub=sub),
        grid=(t // tm,),
        in_specs=[fm_spec, tok(D_MODEL), tok(D_MODEL),
                  _const_spec(w_fo.shape), _const_spec(w_o.shape), _const_spec(g_post.shape)],
        out_specs=tok(D_MODEL),
        out_shape=jax.ShapeDtypeStruct((t, D_MODEL), BF16),
        scratch_shapes=[pltpu.VMEM((BRANCH_WIDTH // LANES, tm, LANES), F32)],
        compiler_params=pltpu.CompilerParams(dimension_semantics=("arbitrary",),
                                             vmem_limit_bytes=VMEM_LIMIT_V7X),
        name="mix_out",
    )(fm, sga, pb, w_fo, w_o, g_post)


HALO = 8
M_HALO = 16
SEG_PAD = 8


def _ffn_prologue(x_ref, m_ref, x1_halo, g_ref, lhs_ref, xs_ref, *, tm):
    seg = tm // SUBLANES
    pitch = seg + SEG_PAD
    n_lane_blocks = D_MODEL // LANES
    x1 = x_ref[...] + m_ref[...].astype(F32)
    for c in range(n_lane_blocks):
        for s in range(SUBLANES):
            xs_ref[c, s * pitch:s * pitch + seg, :] = x1[s * seg:(s + 1) * seg,
                                                        c * LANES:(c + 1) * LANES]
    xp = jnp.concatenate(
        [jnp.concatenate([xs_ref[c, pl.ds(v, SUBLANES, stride=pitch), :]
                          for c in range(n_lane_blocks)], axis=1)
         for v in range(seg)], axis=0)
    g = g_ref[...]
    lhs_ref[0:tm, :] = (xp * _rms_scale(xp) * g).astype(BF16)
    lhs_ref[tm:tm + 2 * HALO, :] = (x1_halo * _rms_scale(x1_halo) * g).astype(BF16)


def _ffn_epilogue(ff, xs_ref, gp_ref, o_ref, ys_ref, sc_ref, *, tm):
    seg = tm // SUBLANES
    pitch = seg + SEG_PAD
    n_lane_blocks = D_MODEL // LANES
    scale = _rms_scale(ff)
    for v in range(seg):
        rows = slice(v * SUBLANES, (v + 1) * SUBLANES)
        for c in range(n_lane_blocks):
            ys_ref[c, pl.ds(v, SUBLANES, stride=pitch), :] = ff[rows, c * LANES:(c + 1) * LANES]
        sc_ref[pl.ds(v, SUBLANES, stride=pitch), :] = jnp.broadcast_to(scale[rows],
                                                                       (SUBLANES, LANES))
    g = gp_ref[...]
    for c in range(n_lane_blocks):
        for s in range(SUBLANES):
            rows = slice(s * pitch, s * pitch + seg)
            o_ref[s * seg:(s + 1) * seg, c * LANES:(c + 1) * LANES] = (
                xs_ref[c, rows, :]
                + ys_ref[c, rows, :] * sc_ref[rows, :] * g[:, c * LANES:(c + 1) * LANES])


def _ffn_kernel(x_ref, xprev_ref, xnext_ref, m_ref, mprev_ref, mnext_ref, g_ref,
                wup_ref, cw_ref, cb_ref, wdn_ref, gp_ref, o_ref, acc_ref, ys_ref, sc_ref, *per_sub,
                sub, fc, subs_per_seq, lookahead, down_group):
    n_sub = x_ref.shape[0] // sub
    lhs_refs, xs_refs = per_sub[:n_sub], per_sub[n_sub:]

    def x1_rows(xr, mr, keep):
        return xr + mr.astype(F32)[keep]

    for a in range(n_sub):
        r0 = a * sub
        rows = pl.ds(r0, sub)
        last8, first8 = slice(M_HALO - HALO, M_HALO), slice(0, HALO)
        if a == 0:
            before = x1_rows(xprev_ref[...], mprev_ref[...], last8)
        else:
            before = x1_rows(x_ref[r0 - HALO:r0, :], m_ref[r0 - M_HALO:r0, :], last8)
        if a == n_sub - 1:
            after = x1_rows(xnext_ref[...], mnext_ref[...], first8)
        else:
            after = x1_rows(x_ref[r0 + sub:r0 + sub + HALO, :],
                            m_ref[r0 + sub:r0 + sub + M_HALO, :], first8)
        _ffn_prologue(x_ref.at[rows], m_ref.at[rows], jnp.concatenate([before, after], axis=0),
                      g_ref, lhs_refs[a], xs_refs[a], tm=sub)
    for a in range(n_sub):
        rows = pl.ds(a * sub, sub)
        ff = _ffn_matmuls(lhs_refs[a], wup_ref, cw_ref, cb_ref, wdn_ref, acc_ref,
                          t_in_seq=(pl.program_id(0) * n_sub + a) % subs_per_seq, tm=sub, fc=fc,
                          tiles_per_seq=subs_per_seq, lookahead=lookahead, down_group=down_group)
        _ffn_epilogue(ff, xs_refs[a], gp_ref, o_ref.at[rows], ys_ref, sc_ref, tm=sub)


def _ffn_matmuls(lhs_ref, wup_ref, cw_ref, cb_ref, wdn_ref, acc_ref,
                 *, t_in_seq, tm, fc, tiles_per_seq, lookahead, down_group):
    keep_prev = (t_in_seq != 0).astype(F32)
    keep_next = (t_in_seq != tiles_per_seq - 1).astype(F32)
    sub = lax.broadcasted_iota(jnp.int32, (SUBLANES, 1), 0)

    def conv(up_all, col0, gain=1.0):
        up = up_all[0:tm]
        prev_row = up_all[tm + HALO - 1:tm + HALO] * keep_prev
        next_row = up_all[tm + HALO:tm + HALO + 1] * keep_next
        head = jnp.where(sub == 0, prev_row, pltpu.roll(up[tm - SUBLANES:tm], 1, axis=0))
        tail = jnp.where(sub == SUBLANES - 1, next_row,
                         pltpu.roll(up[0:SUBLANES], SUBLANES - 1, axis=0))
        up_m = jnp.concatenate([head, up[0:tm - SUBLANES]], axis=0)
        up_p = jnp.concatenate([up[SUBLANES:tm], tail], axis=0)
        w = cw_ref[:, col0:col0 + fc] * gain
        return (up_m * w[0:1] + up * w[1:2] + up_p * w[2:3]) + cb_ref[:, col0:col0 + fc] * gain

    lhs = lhs_ref[...]

    def up_pair(j):
        return (_dot(lhs, wup_ref[:, j * fc:(j + 1) * fc]),
                _dot(lhs, wup_ref[:, D_FF + j * fc:D_FF + (j + 1) * fc]))

    n_chunks = D_FF // fc
    pending = [up_pair(j) for j in range(min(lookahead, n_chunks))]
    acts = []
    for j in range(n_chunks):
        ug, uv = pending.pop(0)
        if j + lookahead < n_chunks:
            pending.append(up_pair(j + lookahead))
        cg = conv(ug, j * fc)
        half_cv = conv(uv, D_FF + j * fc, gain=0.5)
        acts.append((cg * (1.0 + jnp.tanh(_gelu_inner(cg))) * half_cv).astype(BF16))
        if len(acts) == down_group:
            j0 = j + 1 - down_group
            part = _dot(jnp.concatenate(acts, axis=1), wdn_ref[j0 * fc:(j + 1) * fc, :])
            acts = []
            if j + 1 == n_chunks:
                return part if j0 == 0 else acc_ref[...] + part
            if j0 == 0:
                acc_ref[...] = part
            else:
                acc_ref[...] += part


def _ffn(x2d, m2d, g_pre, w_up, conv_w, conv_b, w_down, g_post,
         *, s, tm, sub, fc, lookahead, down_group, vmem_limit):
    t = x2d.shape[0]
    n_sub = tm // sub
    pitch = sub // SUBLANES + SEG_PAD
    perm_scratch = pltpu.VMEM((D_MODEL // LANES, SUBLANES * pitch, LANES), F32)
    tok = pl.BlockSpec((tm, D_MODEL), lambda i: (i, 0))

    def halo_specs(rows):
        per_tile, n_blocks = tm // rows, t // rows
        return (pl.BlockSpec((rows, D_MODEL), lambda i: (jnp.maximum(i * per_tile - 1, 0), 0)),
                pl.BlockSpec((rows, D_MODEL),
                             lambda i: (jnp.minimum((i + 1) * per_tile, n_blocks - 1), 0)))

    x_prev, x_next = halo_specs(HALO)
    m_prev, m_next = halo_specs(M_HALO)
    return pl.pallas_call(
        functools.partial(_ffn_kernel, sub=sub, fc=fc, subs_per_seq=s // sub,
                          lookahead=lookahead, down_group=down_group),
        grid=(t // tm,),
        in_specs=[tok, x_prev, x_next, tok, m_prev, m_next, _const_spec(g_pre.shape),
                  _const_spec(w_up.shape), _const_spec(conv_w.shape), _const_spec(conv_b.shape),
                  _const_spec(w_down.shape), _const_spec(g_post.shape)],
        out_specs=tok,
        out_shape=jax.ShapeDtypeStruct((t, D_MODEL), F32),
        scratch_shapes=([pltpu.VMEM((sub, D_MODEL), F32), perm_scratch,
                         pltpu.VMEM((SUBLANES * pitch, LANES), F32)]
                        + [pltpu.VMEM((sub + 2 * HALO, D_MODEL), BF16)] * n_sub
                        + [perm_scratch] * n_sub),
        compiler_params=pltpu.CompilerParams(dimension_semantics=("arbitrary",),
                                             vmem_limit_bytes=vmem_limit),
        name="ffn",
    )(x2d, x2d, x2d, m2d, m2d, m2d, g_pre, w_up, conv_w, conv_b, w_down, g_post)


def _layer(x, p, consts):
    nb, s, d = x.shape
    x2d = x.reshape(nb * s, d)
    zr, zi, sga, pb = _mix_in(x2d, p["g_pre_mix"], p["w_in"], consts["cs128"], p["ln_g"],
                              p["ln_b"], p["w_s"], p["bias_full"], p["w_so"], tm=1024, sub=512)
    fm = _seq_dft(zr, zi, consts["gmat"][s], nb=nb, s=s, cw=256)
    m = _mix_out(fm, sga, pb, p["w_fo"], p["w_o"], p["g_post_mix"], tm=1024, sub=512,
                 n1=s // DFT_N2)
    y = _ffn(x2d, m, p["g_pre_ffn"], p["w_up"], p["conv_w"], p["conv_b"],
             p["w_down"], p["g_post_ffn"], s=s, tm=1024, sub=512, fc=512, lookahead=3,
             down_group=2, vmem_limit=VMEM_LIMIT_FFN_V7X)
    return y.reshape(nb, s, d)


def kernel(x_prompt, x_sample, norm_pre_mix, w_in, sgu_ln_g, sgu_ln_b, sgu_w_s, sgu_b_s,
           w_fourier_out, w_sgu_out, w_o, norm_post_mix, norm_pre_ffn,
           w_up, conv_w, conv_b, w_down, norm_post_ffn):
    depth = w_in.shape[0]
    consts = {
        "cs128": _channel_dft_matrix(),
        "gmat": {s: _dft_stage_matrix(s) for s in {x_prompt.shape[1], x_sample.shape[1]}},
    }
    y_prompt, y_sample = x_prompt, x_sample
    for l in range(depth):
        row = lambda a: a[l].reshape(1, -1).astype(F32)
        p = {
            "g_pre_mix": row(norm_pre_mix), "w_in": w_in[l].astype(BF16),
            "ln_g": row(sgu_ln_g), "ln_b": row(sgu_ln_b),
            "w_s": sgu_w_s[l].astype(BF16),
            "bias_full": jnp.repeat(sgu_b_s[l].T.astype(F32), GROUP_DIM, axis=1),
            "w_so": w_sgu_out[l].astype(BF16), "w_fo": w_fourier_out[l].astype(BF16),
            "w_o": w_o[l].astype(BF16), "g_post_mix": row(norm_post_mix),
            "g_pre_ffn": row(norm_pre_ffn), "w_up": w_up[l].astype(BF16),
            "conv_w": conv_w[l].astype(F32), "conv_b": row(conv_b),
            "w_down": w_down[l].astype(BF16), "g_post_ffn": row(norm_post_ffn),
        }
        y_prompt = _layer(y_prompt, p, consts)
        y_sample = _layer(y_sample, p, consts)
    return (y_prompt, y_sample)
```

```python
import functools
import math

import jax
import jax.numpy as jnp
from jax import lax
from jax.experimental import pallas as pl
from jax.experimental.pallas import tpu as pltpu

D_MODEL = 1024
N_GROUPS = 4
GROUP_DIM = 128
BRANCH_WIDTH = N_GROUPS * GROUP_DIM
CHUNK = 128
D_FF = 4 * D_MODEL
EPS = 1e-6

O_F, O_U, O_V, O_GA, O_GB = 0, 512, 1024, 1536, 2560

SUBLANES = 8
LANES = 128
DFT_N2 = 512
VMEM_LIMIT_V7X = 56 * 1024 * 1024
VMEM_LIMIT_FFN_V7X = 62 * 1024 * 1024

BF16 = jnp.bfloat16
F32 = jnp.float32


def _const_spec(shape):
    nd = len(shape)
    return pl.BlockSpec(shape, lambda *_: (0,) * nd, pipeline_mode=pl.Buffered(1))


def _rms_scale(x):
    return lax.rsqrt(jnp.mean(x * x, axis=-1, keepdims=True) + EPS)


def _gelu_inner(x):
    k = math.sqrt(2.0 / math.pi)
    return x * (k + (k * 0.044715) * (x * x))


def _gelu_tanh(x):
    return 0.5 * x * (1.0 + jnp.tanh(_gelu_inner(x)))


def _sigmoid(x):
    return 1.0 / (1.0 + jnp.exp(-x))


def _dot(a, b):
    return jnp.dot(a, b, preferred_element_type=F32)


def _mix_in_kernel(x_ref, g_ref, win_ref, cs_ref, lng_ref, lnb_ref, ws_ref, bias_ref, wso_ref,
                   zr_ref, zi_ref, sga_ref, pb_ref, *, sub):
    for a in range(x_ref.shape[0] // sub):
        rows = pl.ds(a * sub, sub)
        _mix_in_tile(x_ref.at[rows], g_ref, win_ref, cs_ref, lng_ref, lnb_ref, ws_ref, bias_ref,
                     wso_ref, zr_ref.at[rows], zi_ref.at[rows], sga_ref.at[rows], pb_ref.at[rows],
                     tm=sub)


def _mix_in_tile(x_ref, g_ref, win_ref, cs_ref, lng_ref, lnb_ref, ws_ref, bias_ref, wso_ref,
                 zr_ref, zi_ref, sga_ref, pb_ref, *, tm):
    x = x_ref[...]
    h = (x * _rms_scale(x) * g_ref[...]).astype(BF16)

    v_pre = _dot(h, win_ref[:, O_V:O_V + BRANCH_WIDTH])
    u_pre = _dot(h, win_ref[:, O_U:O_U + BRANCH_WIDTH])
    f = _dot(h, win_ref[:, O_F:O_F + BRANCH_WIDTH]).astype(BF16)
    gate_a_pre = _dot(h, win_ref[:, O_GA:O_GA + D_MODEL])

    v = _gelu_tanh(v_pre)
    mu = jnp.mean(v, axis=-1, keepdims=True)
    vc = v - mu
    var = jnp.mean(vc * vc, axis=-1, keepdims=True)
    vn = (vc * lax.rsqrt(var + EPS) * lng_ref[...] + lnb_ref[...]).astype(BF16)

    for g in range(N_GROUPS):
        sl = slice(g * GROUP_DIM, (g + 1) * GROUP_DIM)
        z = _dot(f[:, sl], cs_ref[...])
        zr_ref[:, sl] = z[:, :GROUP_DIM].astype(BF16)
        zi_ref[:, sl] = z[:, GROUP_DIM:].astype(BF16)
    sga_ref[...] = _sigmoid(gate_a_pre).astype(BF16)

    n_chunks = tm // CHUNK
    per_group = []
    for g in range(N_GROUPS):
        rhs = jnp.concatenate(
            [vn[n * CHUNK:(n + 1) * CHUNK, g * GROUP_DIM:(g + 1) * GROUP_DIM]
             for n in range(n_chunks)], axis=1)
        per_group.append(_dot(ws_ref[g], rhs))
    gate_b_pre = _dot(h, win_ref[:, O_GB:O_GB + D_MODEL])
    mixed = jnp.concatenate(
        [jnp.concatenate([per_group[g][:, n * GROUP_DIM:(n + 1) * GROUP_DIM]
                          for g in range(N_GROUPS)], axis=1) + bias_ref[...]
         for n in range(n_chunks)], axis=0)
    gated = (_gelu_tanh(u_pre) * mixed).astype(BF16)
    y_b = _dot(gated, wso_ref[...])
    pb_ref[...] = (_sigmoid(gate_b_pre) * y_b).astype(BF16)


def _mix_in(x2d, g_pre, w_in, cs128, ln_g, ln_b, w_s, bias_full, w_so, *, tm, sub):
    t = x2d.shape[0]
    tok = lambda w: pl.BlockSpec((tm, w), lambda i: (i, 0))
    return pl.pallas_call(
        functools.partial(_mix_in_kernel, sub=sub),
        grid=(t // tm,),
        in_specs=[tok(D_MODEL), _const_spec(g_pre.shape), _const_spec(w_in.shape),
                  _const_spec(cs128.shape), _const_spec(ln_g.shape), _const_spec(ln_b.shape),
                  _const_spec(w_s.shape), _const_spec(bias_full.shape), _const_spec(w_so.shape)],
        out_specs=[tok(BRANCH_WIDTH), tok(BRANCH_WIDTH), tok(D_MODEL), tok(D_MODEL)],
        out_shape=[jax.ShapeDtypeStruct((t, BRANCH_WIDTH), BF16),
                   jax.ShapeDtypeStruct((t, BRANCH_WIDTH), BF16),
                   jax.ShapeDtypeStruct((t, D_MODEL), BF16),
                   jax.ShapeDtypeStruct((t, D_MODEL), BF16)],
        compiler_params=pltpu.CompilerParams(dimension_semantics=("arbitrary",),
                                             vmem_limit_bytes=VMEM_LIMIT_V7X),
        name="mix_in",
    )(x2d, g_pre, w_in, cs128, ln_g, ln_b, w_s, bias_full, w_so)


def _cadd(a, b):
    return (a[0] + b[0], a[1] + b[1])


def _csub(a, b):
    return (a[0] - b[0], a[1] - b[1])


def _cmul_const(a, wr, wi):
    return (a[0] * wr - a[1] * wi, a[0] * wi + a[1] * wr)


def _dft4(x0, x1, x2, x3):
    t0, t1 = _cadd(x0, x2), _csub(x0, x2)
    t2, t3 = _cadd(x1, x3), _csub(x1, x3)
    mi_t3 = (t3[1], -t3[0])
    return _cadd(t0, t2), _cadd(t1, mi_t3), _csub(t0, t2), _csub(t1, mi_t3)


def _dft_blocks(xs):
    n = len(xs)
    if n == 4:
        return list(_dft4(*xs))
    assert n == 16
    inner = [_dft4(xs[n2], xs[4 + n2], xs[8 + n2], xs[12 + n2]) for n2 in range(4)]
    out = [None] * 16
    for k1 in range(4):
        tw = []
        for n2 in range(4):
            m = (n2 * k1) % 16
            val = inner[n2][k1]
            if m != 0:
                ang = -2.0 * math.pi * m / 16.0
                val = _cmul_const(val, math.cos(ang), math.sin(ang))
            tw.append(val)
        outer = _dft4(*tw)
        for k2 in range(4):
            out[k1 + 4 * k2] = outer[k2]
    return out


def _seq_dft_kernel(zr_ref, zi_ref, g_ref, o_ref, y_ref, *, n1, rb, cw, unroll):
    for part in range(o_ref.shape[0]):
        lanes = slice(part * cw, (part + 1) * cw)

        def butterflies(r0, lanes=lanes, part=part):
            xs = []
            for b in range(n1):
                rows = pl.ds(b * DFT_N2 + r0, rb)
                xs.append((zr_ref[rows, lanes].astype(F32), zi_ref[rows, lanes].astype(F32)))
            ys = _dft_blocks(xs)
            for b in range(n1):
                y_ref[part, b, pl.ds(r0, rb), :] = ys[b][0].astype(BF16)
                y_ref[part, b, pl.ds(DFT_N2 + r0, rb), :] = ys[b][1].astype(BF16)

        if unroll:
            for r in range(DFT_N2 // rb):
                butterflies(r * rb)
        else:
            def body(r, carry, butterflies=butterflies):
                butterflies(pl.multiple_of(r * rb, rb))
                return carry
            lax.fori_loop(0, DFT_N2 // rb, body, 0)

        for k1 in range(n1):
            o_ref[part, :, k1 * cw:(k1 + 1) * cw] = _dot(g_ref[k1], y_ref[part, k1]).astype(BF16)


def _seq_dft(zr, zi, gmat, *, nb, s, cw, parts_per_step):
    n1 = s // DFT_N2
    nh = BRANCH_WIDTH // cw
    width = cw * parts_per_step
    return pl.pallas_call(
        functools.partial(_seq_dft_kernel, n1=n1, rb=16, cw=cw, unroll=parts_per_step > 1),
        grid=(nb, nh // parts_per_step),
        in_specs=[pl.BlockSpec((s, width), lambda b, h: (b, h)),
                  pl.BlockSpec((s, width), lambda b, h: (b, h)),
                  _const_spec(gmat.shape)],
        out_specs=pl.BlockSpec((None, parts_per_step, DFT_N2, n1 * cw), lambda b, h: (b, h, 0, 0)),
        out_shape=jax.ShapeDtypeStruct((nb, nh, DFT_N2, n1 * cw), BF16),
        scratch_shapes=[pltpu.VMEM((parts_per_step, n1, 2 * DFT_N2, cw), BF16)],
        compiler_params=pltpu.CompilerParams(
            dimension_semantics=("arbitrary", "arbitrary"),
            vmem_limit_bytes=VMEM_LIMIT_V7X),
        name="seq_dft",
    )(zr, zi, gmat)


def _dft_stage_matrix(s):
    n1 = s // DFT_N2
    j = jnp.arange(DFT_N2, dtype=jnp.int32)
    ma = jnp.bitwise_and(jnp.arange(n1, dtype=jnp.int32)[:, None] * j[None, :], s - 1)
    a = ma.astype(F32) * (2.0 * math.pi / s)
    mb = jnp.bitwise_and(j[:, None] * j[None, :], DFT_N2 - 1)
    b = mb.astype(F32) * (2.0 * math.pi / DFT_N2)
    scale = 1.0 / math.sqrt(s)
    ca, sa = (jnp.cos(a) * scale)[:, None, :], (jnp.sin(a) * scale)[:, None, :]
    cb, sb = jnp.cos(b)[None], jnp.sin(b)[None]
    return jnp.concatenate([ca * cb - sa * sb, sa * cb + ca * sb], axis=-1).astype(BF16)


def _channel_dft_matrix():
    c = jnp.arange(GROUP_DIM, dtype=jnp.int32)
    m = jnp.bitwise_and(c[:, None] * c[None, :], GROUP_DIM - 1).astype(F32)
    th = m * (2.0 * math.pi / GROUP_DIM)
    scale = 1.0 / math.sqrt(GROUP_DIM)
    return jnp.concatenate([jnp.cos(th) * scale, -jnp.sin(th) * scale], axis=-1).astype(BF16)


def _unpermute_dft(fm_ref, fs_ref, n1):
    nh, rows, width = fm_ref.shape
    cw = width // n1
    for h in range(nh):
        for k1 in range(n1):
            for c in range(cw // LANES):
                col = k1 * cw + c * LANES
                fs_ref[h * (cw // LANES) + c, pl.ds(k1, rows, stride=n1), :] = (
                    fm_ref[h, :, col:col + LANES].astype(F32))
    return jnp.concatenate([fs_ref[c] for c in range(BRANCH_WIDTH // LANES)], axis=1).astype(BF16)


def _mix_out_kernel(fm_ref, sga_ref, pb_ref, wfo_ref, wo_ref, g_ref, o_ref, fs_ref, *, n1, sub):
    fm = _unpermute_dft(fm_ref, fs_ref, n1)
    for a in range(fm.shape[0] // sub):
        rows = slice(a * sub, (a + 1) * sub)
        y_a = _dot(fm[rows], wfo_ref[...])
        merged = sga_ref[rows, :].astype(F32) * y_a + pb_ref[rows, :].astype(F32)
        o = _dot(merged.astype(BF16), wo_ref[...])
        o_ref[rows, :] = (o * _rms_scale(o) * g_ref[...]).astype(BF16)


def _mix_out(fm, sga, pb, w_fo, w_o, g_post, *, tm, sub, n1):
    t = sga.shape[0]
    _, nh, _, width = fm.shape
    tiles_per_seq = n1 * DFT_N2 // tm
    tok = lambda w: pl.BlockSpec((tm, w), lambda i: (i, 0))
    fm_spec = pl.BlockSpec((None, nh, tm // n1, width),
                           lambda i: (i // tiles_per_seq, 0, i % tiles_per_seq, 0))
    return pl.pallas_call(
        functools.partial(_mix_out_kernel, n1=n1, sub=sub),
        grid=(t // tm,),
        in_specs=[fm_spec, tok(D_MODEL), tok(D_MODEL),
                  _const_spec(w_fo.shape), _const_spec(w_o.shape), _const_spec(g_post.shape)],
        out_specs=tok(D_MODEL),
        out_shape=jax.ShapeDtypeStruct((t, D_MODEL), BF16),
        scratch_shapes=[pltpu.VMEM((BRANCH_WIDTH // LANES, tm, LANES), F32)],
        compiler_params=pltpu.CompilerParams(dimension_semantics=("arbitrary",),
                                             vmem_limit_bytes=VMEM_LIMIT_V7X),
        name="mix_out",
    )(fm, sga, pb, w_fo, w_o, g_post)


HALO = 8
M_HALO = 16
SEG_PAD = 8


def _ffn_prologue(x_ref, m_ref, x1_halo, g_ref, lhs_ref, xs_ref, *, tm):
    seg = tm // SUBLANES
    pitch = seg + SEG_PAD
    n_lane_blocks = D_MODEL // LANES
    x1 = x_ref[...] + m_ref[...].astype(F32)
    for c in range(n_lane_blocks):
        for s in range(SUBLANES):
            xs_ref[c, s * pitch:s * pitch + seg, :] = x1[s * seg:(s + 1) * seg,
                                                        c * LANES:(c + 1) * LANES]
    xp = jnp.concatenate(
        [jnp.concatenate([xs_ref[c, pl.ds(v, SUBLANES, stride=pitch), :]
                          for c in range(n_lane_blocks)], axis=1)
         for v in range(seg)], axis=0)
    g = g_ref[...]
    lhs_ref[0:tm, :] = (xp * _rms_scale(xp) * g).astype(BF16)
    lhs_ref[tm:tm + 2 * HALO, :] = (x1_halo * _rms_scale(x1_halo) * g).astype(BF16)


def _ffn_epilogue(ff, xs_ref, gp_ref, o_ref, ys_ref, sc_ref, *, tm):
    seg = tm // SUBLANES
    pitch = seg + SEG_PAD
    n_lane_blocks = D_MODEL // LANES
    scale = _rms_scale(ff)
    for v in range(seg):
        rows = slice(v * SUBLANES, (v + 1) * SUBLANES)
        for c in range(n_lane_blocks):
            ys_ref[c, pl.ds(v, SUBLANES, stride=pitch), :] = ff[rows, c * LANES:(c + 1) * LANES]
        sc_ref[pl.ds(v, SUBLANES, stride=pitch), :] = jnp.broadcast_to(scale[rows],
                                                                       (SUBLANES, LANES))
    g = gp_ref[...]
    for c in range(n_lane_blocks):
        for s in range(SUBLANES):
            rows = slice(s * pitch, s * pitch + seg)
            o_ref[s * seg:(s + 1) * seg, c * LANES:(c + 1) * LANES] = (
                xs_ref[c, rows, :]
                + ys_ref[c, rows, :] * sc_ref[rows, :] * g[:, c * LANES:(c + 1) * LANES])


def _ffn_kernel(x_ref, xprev_ref, xnext_ref, m_ref, mprev_ref, mnext_ref, g_ref,
                wup_ref, cw_ref, cb_ref, wdn_ref, gp_ref, o_ref, acc_ref, ys_ref, sc_ref, *per_sub,
                sub, fc, subs_per_seq, lookahead, down_group):
    n_sub = x_ref.shape[0] // sub
    lhs_refs, xs_refs = per_sub[:n_sub], per_sub[n_sub:]

    def x1_rows(xr, mr, keep):
        return xr + mr.astype(F32)[keep]

    for a in range(n_sub):
        r0 = a * sub
        rows = pl.ds(r0, sub)
        last8, first8 = slice(M_HALO - HALO, M_HALO), slice(0, HALO)
        if a == 0:
            before = x1_rows(xprev_ref[...], mprev_ref[...], last8)
        else:
            before = x1_rows(x_ref[r0 - HALO:r0, :], m_ref[r0 - M_HALO:r0, :], last8)
        if a == n_sub - 1:
            after = x1_rows(xnext_ref[...], mnext_ref[...], first8)
        else:
            after = x1_rows(x_ref[r0 + sub:r0 + sub + HALO, :],
                            m_ref[r0 + sub:r0 + sub + M_HALO, :], first8)
        _ffn_prologue(x_ref.at[rows], m_ref.at[rows], jnp.concatenate([before, after], axis=0),
                      g_ref, lhs_refs[a], xs_refs[a], tm=sub)
    for a in range(n_sub):
        rows = pl.ds(a * sub, sub)
        ff = _ffn_matmuls(lhs_refs[a], wup_ref, cw_ref, cb_ref, wdn_ref, acc_ref,
                          t_in_seq=(pl.program_id(0) * n_sub + a) % subs_per_seq, tm=sub, fc=fc,
                          tiles_per_seq=subs_per_seq, lookahead=lookahead, down_group=down_group)
        _ffn_epilogue(ff, xs_refs[a], gp_ref, o_ref.at[rows], ys_ref, sc_ref, tm=sub)


def _ffn_matmuls(lhs_ref, wup_ref, cw_ref, cb_ref, wdn_ref, acc_ref,
                 *, t_in_seq, tm, fc, tiles_per_seq, lookahead, down_group):
    keep_prev = (t_in_seq != 0).astype(F32)
    keep_next = (t_in_seq != tiles_per_seq - 1).astype(F32)
    sub = lax.broadcasted_iota(jnp.int32, (SUBLANES, 1), 0)

    def conv(up_all, col0, gain=1.0):
        up = up_all[0:tm]
        prev_row = up_all[tm + HALO - 1:tm + HALO] * keep_prev
        next_row = up_all[tm + HALO:tm + HALO + 1] * keep_next
        head = jnp.where(sub == 0, prev_row, pltpu.roll(up[tm - SUBLANES:tm], 1, axis=0))
        tail = jnp.where(sub == SUBLANES - 1, next_row,
                         pltpu.roll(up[0:SUBLANES], SUBLANES - 1, axis=0))
        up_m = jnp.concatenate([head, up[0:tm - SUBLANES]], axis=0)
        up_p = jnp.concatenate([up[SUBLANES:tm], tail], axis=0)
        w = cw_ref[:, col0:col0 + fc] * gain
        return (up_m * w[0:1] + up * w[1:2] + up_p * w[2:3]) + cb_ref[:, col0:col0 + fc] * gain

    lhs = lhs_ref[...]

    def up_pair(j):
        return (_dot(lhs, wup_ref[:, j * fc:(j + 1) * fc]),
                _dot(lhs, wup_ref[:, D_FF + j * fc:D_FF + (j + 1) * fc]))

    n_chunks = D_FF // fc
    pending = [up_pair(j) for j in range(min(lookahead, n_chunks))]
    acts = []
    for j in range(n_chunks):
        ug, uv = pending.pop(0)
        if j + lookahead < n_chunks:
            pending.append(up_pair(j + lookahead))
        cg = conv(ug, j * fc)
        half_cv = conv(uv, D_FF + j * fc, gain=0.5)
        acts.append((cg * (1.0 + jnp.tanh(_gelu_inner(cg))) * half_cv).astype(BF16))
        if len(acts) == down_group:
            j0 = j + 1 - down_group
            part = _dot(jnp.concatenate(acts, axis=1), wdn_ref[j0 * fc:(j + 1) * fc, :])
            acts = []
            if j + 1 == n_chunks:
                return part if j0 == 0 else acc_ref[...] + part
            if j0 == 0:
                acc_ref[...] = part
            else:
                acc_ref[...] += part


def _ffn(x2d, m2d, g_pre, w_up, conv_w, conv_b, w_down, g_post,
         *, s, tm, sub, fc, lookahead, down_group, vmem_limit):
    t = x2d.shape[0]
    n_sub = tm // sub
    pitch = sub // SUBLANES + SEG_PAD
    perm_scratch = pltpu.VMEM((D_MODEL // LANES, SUBLANES * pitch, LANES), F32)
    tok = pl.BlockSpec((tm, D_MODEL), lambda i: (i, 0))

    def halo_specs(rows):
        per_tile, n_blocks = tm // rows, t // rows
        return (pl.BlockSpec((rows, D_MODEL), lambda i: (jnp.maximum(i * per_tile - 1, 0), 0)),
                pl.BlockSpec((rows, D_MODEL),
                             lambda i: (jnp.minimum((i + 1) * per_tile, n_blocks - 1), 0)))

    x_prev, x_next = halo_specs(HALO)
    m_prev, m_next = halo_specs(M_HALO)
    return pl.pallas_call(
        functools.partial(_ffn_kernel, sub=sub, fc=fc, subs_per_seq=s // sub,
                          lookahead=lookahead, down_group=down_group),
        grid=(t // tm,),
        in_specs=[tok, x_prev, x_next, tok, m_prev, m_next, _const_spec(g_pre.shape),
                  _const_spec(w_up.shape), _const_spec(conv_w.shape), _const_spec(conv_b.shape),
                  _const_spec(w_down.shape), _const_spec(g_post.shape)],
        out_specs=tok,
        out_shape=jax.ShapeDtypeStruct((t, D_MODEL), F32),
        scratch_shapes=([pltpu.VMEM((sub, D_MODEL), F32), perm_scratch,
                         pltpu.VMEM((SUBLANES * pitch, LANES), F32)]
                        + [pltpu.VMEM((sub + 2 * HALO, D_MODEL), BF16)] * n_sub
                        + [perm_scratch] * n_sub),
        compiler_params=pltpu.CompilerParams(dimension_semantics=("arbitrary",),
                                             vmem_limit_bytes=vmem_limit),
        name="ffn",
    )(x2d, x2d, x2d, m2d, m2d, m2d, g_pre, w_up, conv_w, conv_b, w_down, g_post)


def _layer(x, p, consts):
    nb, s, d = x.shape
    x2d = x.reshape(nb * s, d)
    zr, zi, sga, pb = _mix_in(x2d, p["g_pre_mix"], p["w_in"], consts["cs128"], p["ln_g"],
                              p["ln_b"], p["w_s"], p["bias_full"], p["w_so"], tm=1024, sub=512)
    fm = _seq_dft(zr, zi, consts["gmat"][s], nb=nb, s=s, cw=256,
                  parts_per_step=2 if s <= 2048 else 1)
    m = _mix_out(fm, sga, pb, p["w_fo"], p["w_o"], p["g_post_mix"], tm=1024, sub=512,
                 n1=s // DFT_N2)
    y = _ffn(x2d, m, p["g_pre_ffn"], p["w_up"], p["conv_w"], p["conv_b"],
             p["w_down"], p["g_post_ffn"], s=s, tm=1024, sub=512, fc=512, lookahead=3,
             down_group=2, vmem_limit=VMEM_LIMIT_FFN_V7X)
    return y.reshape(nb, s, d)


def kernel(x_prompt, x_sample, norm_pre_mix, w_in, sgu_ln_g, sgu_ln_b, sgu_w_s, sgu_b_s,
           w_fourier_out, w_sgu_out, w_o, norm_post_mix, norm_pre_ffn,
           w_up, conv_w, conv_b, w_down, norm_post_ffn):
    depth = w_in.shape[0]
    consts = {
        "cs128": _channel_dft_matrix(),
        "gmat": {s: _dft_stage_matrix(s) for s in {x_prompt.shape[1], x_sample.shape[1]}},
    }
    y_prompt, y_sample = x_prompt, x_sample
    for l in range(depth):
        row = lambda a: a[l].reshape(1, -1).astype(F32)
        p = {
            "g_pre_mix": row(norm_pre_mix), "w_in": w_in[l].astype(BF16),
            "ln_g": row(sgu_ln_g), "ln_b": row(sgu_ln_b),
            "w_s": sgu_w_s[l].astype(BF16),
            "bias_full": jnp.repeat(sgu_b_s[l].T.astype(F32), GROUP_DIM, axis=1),
            "w_so": w_sgu_out[l].astype(BF16), "w_fo": w_fourier_out[l].astype(BF16),
            "w_o": w_o[l].astype(BF16), "g_post_mix": row(norm_post_mix),
            "g_pre_ffn": row(norm_pre_ffn), "w_up": w_up[l].astype(BF16),
            "conv_w": conv_w[l].astype(F32), "conv_b": row(conv_b),
            "w_down": w_down[l].astype(BF16), "g_post_ffn": row(norm_post_ffn),
        }
        y_prompt = _layer(y_prompt, p, consts)
        y_sample = _layer(y_sample, p, consts)
    return (y_prompt, y_sample)
```
